```python
import math
import jax, jax.numpy as jnp
from jax import lax
import numpy as np

D_MODEL = 1024
BATCH = 2
SEQ = 16384
DEPTH = 1
DEC_BATCH = 128
DEC_SEQ = 4
PAST_LEN = 8192
PAGE_SIZE = 128

D_INNER = 1 * D_MODEL
SSM_HEAD_DIM = 64
N_SSM_HEADS = D_INNER // SSM_HEAD_DIM
N_GROUPS = 2
HEADS_PER_GROUP = N_SSM_HEADS // N_GROUPS
D_STATE = 128
SSM_CONV = 4
CONV_DIM = D_INNER + 2 * N_GROUPS * D_STATE
SSD_CHUNK = 128
N_ATT_HEADS = 8
ATT_HEAD_DIM = 64
D_ATT = N_ATT_HEADS * ATT_HEAD_DIM
Q_BLOCK = 128
D_FF = 2816
FFN_CONV = 3
DN_ALPHA = (2.0 * DEPTH) ** 0.25
DN_BETA = (8.0 * DEPTH) ** -0.25
LN_EPS = 1e-5
RMS_EPS = 1e-5
Z_END = D_INNER
XBC_END = Z_END + CONV_DIM
DT_END = XBC_END + N_SSM_HEADS
Q_END = DT_END + D_ATT
K_END = Q_END + D_ATT
V_END = K_END + D_ATT
F_END = V_END + N_ATT_HEADS
N_IN = F_END + 2 * D_MODEL

kernel_name = 'hybrid_ssd_fox_convffn_step'


def _layer_norm(x, g, b):
    xf = x.astype(jnp.float32)
    mu = jnp.mean(xf, axis=-1, keepdims=True)
    var = jnp.mean(jnp.square(xf - mu), axis=-1, keepdims=True)
    return ((xf - mu) * lax.rsqrt(var + LN_EPS) * g + b).astype(x.dtype)


def _causal_dwconv(full, w, b):
    width = w.shape[0]
    length = full.shape[1] - (width - 1)
    out = b
    for j in range(width):
        out = out + full[:, j:j + length] * w[j]
    return out


def _ssd(x, a, b_mat, c_mat, h0, chunk):
    n, L = x.shape[:2]
    nc = L // chunk
    G, R = N_GROUPS, HEADS_PER_GROUP
    x = x.reshape(n, nc, chunk, G, R, SSM_HEAD_DIM)
    a = a.reshape(n, nc, chunk, G, R).transpose(0, 1, 3, 4, 2)
    b_mat = b_mat.reshape(n, nc, chunk, G, D_STATE)
    c_mat = c_mat.reshape(n, nc, chunk, G, D_STATE)
    a_cs = jnp.cumsum(a, axis=-1)
    causal = jnp.tril(jnp.ones((chunk, chunk), dtype=bool))
    seg = jnp.where(causal, a_cs[..., :, None] - a_cs[..., None, :], -jnp.inf)
    decay = jnp.exp(seg)
    cb = jnp.einsum('nclgd,ncsgd->ncgls', c_mat, b_mat)
    y_diag = jnp.einsum('ncgrls,ncsgrp->nclgrp', cb[:, :, :, None] * decay, x)
    decay_to_end = jnp.exp(a_cs[..., -1:] - a_cs).transpose(0, 1, 4, 2, 3)
    chunk_states = jnp.einsum('nclgd,nclgrp->ncgrpd', b_mat, x * decay_to_end[..., None])
    chunk_decay = jnp.exp(a_cs[..., -1])

    def step(h, inp):
        s_c, d_c = inp
        return d_c[..., None, None] * h + s_c, h

    h_init = h0.reshape(n, G, R, SSM_HEAD_DIM, D_STATE)
    h_final, h_prev = lax.scan(step, h_init, (chunk_states.transpose(1, 0, 2, 3, 4, 5), chunk_decay.transpose(1, 0, 2, 3)))
    h_prev = h_prev.transpose(1, 0, 2, 3, 4, 5)
    y_off = jnp.einsum('nclgd,ncgrpd->nclgrp', c_mat, h_prev) * jnp.exp(a_cs).transpose(0, 1, 4, 2, 3)[..., None]
    y = (y_diag + y_off).reshape(n, L, N_SSM_HEADS, SSM_HEAD_DIM)
    return y, h_final.reshape(n, N_SSM_HEADS, SSM_HEAD_DIM, D_STATE)


def _ssm_branch(z, xbc, dt_raw, conv_prev, h0, conv_w, conv_b, dt_bias, a_log, d_skip, norm_w):
    n, L = z.shape[:2]
    full = jnp.concatenate([conv_prev.astype(xbc.dtype), xbc], axis=1)
    new_conv = full[:, -(SSM_CONV - 1):]
    xbc = jax.nn.silu(_causal_dwconv(full, conv_w, conv_b))
    xs, b_mat, c_mat = jnp.split(xbc, [D_INNER, D_INNER + N_GROUPS * D_STATE], axis=-1)
    xs = xs.reshape(n, L, N_SSM_HEADS, SSM_HEAD_DIM).astype(jnp.float32)
    b_mat = b_mat.reshape(n, L, N_GROUPS, D_STATE).astype(jnp.float32)
    c_mat = c_mat.reshape(n, L, N_GROUPS, D_STATE).astype(jnp.float32)
    dt = jax.nn.softplus(dt_raw.astype(jnp.float32) + dt_bias)
    a = -jnp.exp(a_log.astype(jnp.float32))
    chunk = SSD_CHUNK if L % SSD_CHUNK == 0 else L
    y, h_new = _ssd(xs * dt[..., None], dt * a, b_mat, c_mat, h0.astype(jnp.float32), chunk)
    y = y + xs * d_skip[:, None]
    y = y.reshape(n, L, D_INNER) * jax.nn.silu(z.astype(jnp.float32))
    yg = y.reshape(n, L, N_GROUPS, D_INNER // N_GROUPS)
    yg = yg * lax.rsqrt(jnp.mean(jnp.square(yg), axis=-1, keepdims=True) + RMS_EPS)
    y = yg.reshape(n, L, D_INNER) * norm_w
    return y.astype(z.dtype), new_conv, h_new.astype(h0.dtype)


def _fox_prompt(q, k, v, logf):
    n, L = q.shape[:2]
    nb = L // Q_BLOCK
    scale = ATT_HEAD_DIM ** -0.5
    c = jnp.cumsum(logf, axis=1)
    c_keys = c.transpose(0, 2, 1)[:, :, None, :]
    key_pos = jnp.arange(L)
    q_blocks = q.reshape(n, nb, Q_BLOCK, N_ATT_HEADS, ATT_HEAD_DIM).transpose(1, 0, 2, 3, 4)
    c_blocks = c.reshape(n, nb, Q_BLOCK, N_ATT_HEADS).transpose(1, 0, 3, 2)

    def block(args):
        q_i, c_i, i = args
        s = jnp.einsum('nqhe,nkhe->nhqk', q_i, k, preferred_element_type=jnp.float32) * scale
        s = s + c_i[..., None] - c_keys
        q_pos = i * Q_BLOCK + jnp.arange(Q_BLOCK)
        s = jnp.where(key_pos[None, :] <= q_pos[:, None], s, -jnp.inf)
        p = jax.nn.softmax(s, axis=-1)
        return jnp.einsum('nhqk,nkhe->nqhe', p.astype(v.dtype), v)

    out = lax.map(block, (q_blocks, c_blocks, jnp.arange(nb)))
    return out.transpose(1, 0, 2, 3, 4).reshape(n, L, N_ATT_HEADS, ATT_HEAD_DIM)


def _fox_sample(q, k, v, logf, k_past, v_past, logf_past):
    T = q.shape[1]
    P = k_past.shape[1]
    scale = ATT_HEAD_DIM ** -0.5
    lf_past = logf_past.astype(jnp.float32)
    r_past = lax.cumsum(lf_past, axis=1, reverse=True) - lf_past
    c_new = jnp.cumsum(logf, axis=1).transpose(0, 2, 1)
    s_past = jnp.einsum('nthe,nshe->nhts', q, k_past, preferred_element_type=jnp.float32) * scale
    s_past = s_past + c_new[..., None] + r_past.transpose(0, 2, 1)[:, :, None, :]
    s_new = jnp.einsum('nthe,nshe->nhts', q, k, preferred_element_type=jnp.float32) * scale
    s_new = s_new + c_new[..., :, None] - c_new[..., None, :]
    s_new = jnp.where(jnp.tril(jnp.ones((T, T), dtype=bool)), s_new, -jnp.inf)
    p = jax.nn.softmax(jnp.concatenate([s_past, s_new], axis=-1), axis=-1)
    out = jnp.einsum('nhts,nshe->nthe', p[..., :P].astype(v.dtype), v_past)
    return out + jnp.einsum('nhts,nshe->nthe', p[..., P:].astype(v.dtype), v)


def _layer(x, conv_ssm_prev, ssm_prev, conv_ffn_prev, past, w_in, conv_ssm_w, conv_ssm_b, dt_bias, a_log, d_skip, ssm_norm_w, f_bias, gate_bias, w_ssm_o, w_att_o, w_o, ln1_g, ln1_b, w_ffn_up, ffn_conv_w, ffn_conv_b, w_ffn_down, ln2_g, ln2_b):
    n, L, _ = x.shape
    proj = jnp.einsum('nld,de->nle', x, w_in)
    z, xbc, dt_raw, q, k, v, f_raw, gates = jnp.split(proj, [Z_END, XBC_END, DT_END, Q_END, K_END, V_END, F_END], axis=-1)
    y_ssm, new_conv_ssm, new_ssm = _ssm_branch(z, xbc, dt_raw, conv_ssm_prev, ssm_prev, conv_ssm_w, conv_ssm_b, dt_bias, a_log, d_skip, ssm_norm_w)
    q = q.reshape(n, L, N_ATT_HEADS, ATT_HEAD_DIM)
    k = k.reshape(n, L, N_ATT_HEADS, ATT_HEAD_DIM)
    v = v.reshape(n, L, N_ATT_HEADS, ATT_HEAD_DIM)
    logf = jax.nn.log_sigmoid(f_raw.astype(jnp.float32) + f_bias)
    if past is None:
        att = _fox_prompt(q, k, v, logf)
    else:
        att = _fox_sample(q, k, v, logf, past[0], past[1], past[2])
    g_ssm, g_att = jnp.split(jax.nn.sigmoid(gates + gate_bias), 2, axis=-1)
    merged = g_ssm * (y_ssm @ w_ssm_o) + g_att * (att.reshape(n, L, D_ATT) @ w_att_o)
    h = _layer_norm(DN_ALPHA * x + merged @ w_o, ln1_g, ln1_b)
    gate_h, val_h = jnp.split(h @ w_ffn_up, 2, axis=-1)
    full = jnp.concatenate([conv_ffn_prev.astype(gate_h.dtype), gate_h], axis=1)
    new_conv_ffn = full[:, -(FFN_CONV - 1):]
    ff = jax.nn.gelu(_causal_dwconv(full, ffn_conv_w, ffn_conv_b), approximate=False) * val_h
    out = _layer_norm(DN_ALPHA * h + ff @ w_ffn_down, ln2_g, ln2_b)
    return out, (k, v, logf.astype(x.dtype), new_conv_ssm, new_ssm, new_conv_ffn)


def setup_inputs(seed: int = 0) -> dict:
    key = jax.random.key(seed)
    ks = jax.random.split(key, 32)
    f32 = jnp.float32
    n_pages = PAST_LEN // PAGE_SIZE
    n_pool = (DEC_BATCH * n_pages * 5) // 4

    def nrm(k, shape, scale):
        return jax.random.normal(k, shape, f32) * scale

    x_prompt = nrm(ks[0], (BATCH, SEQ, D_MODEL), 1.0)
    x_sample = nrm(ks[1], (DEC_BATCH, DEC_SEQ, D_MODEL), 1.0)
    cache_k = nrm(ks[2], (DEPTH, n_pool, PAGE_SIZE, N_ATT_HEADS, ATT_HEAD_DIM), 1.0)
    cache_v = nrm(ks[3], (DEPTH, n_pool, PAGE_SIZE, N_ATT_HEADS, ATT_HEAD_DIM), 1.0)
    cache_head_bias = jax.random.uniform(ks[29], (DEPTH, 1, 1, N_ATT_HEADS), f32, 4.0, 9.0)
    cache_logf = jax.nn.log_sigmoid(cache_head_bias + nrm(ks[4], (DEPTH, n_pool, PAGE_SIZE, N_ATT_HEADS), 0.5))
    state_conv_ssm = nrm(ks[5], (DEPTH, DEC_BATCH, SSM_CONV - 1, CONV_DIM), 1.0)
    state_ssm = nrm(ks[6], (DEPTH, DEC_BATCH, N_SSM_HEADS, SSM_HEAD_DIM, D_STATE), 0.1)
    state_conv_ffn = nrm(ks[7], (DEPTH, DEC_BATCH, FFN_CONV - 1, D_FF), 1.0)
    page_table = jax.random.permutation(ks[8], n_pool)[:DEC_BATCH * n_pages].reshape(DEC_BATCH, n_pages).astype(jnp.int32)
    w_in = nrm(ks[9], (DEPTH, D_MODEL, N_IN), D_MODEL ** -0.5)
    w_in = w_in.at[:, :, K_END:V_END].multiply(DN_BETA)
    conv_ssm_w = nrm(ks[10], (DEPTH, SSM_CONV, CONV_DIM), SSM_CONV ** -0.5)
    conv_ssm_b = nrm(ks[11], (DEPTH, CONV_DIM), 0.02)
    dt0 = jnp.exp(jax.random.uniform(ks[12], (DEPTH, N_SSM_HEADS), f32, math.log(1e-3), math.log(1e-1)))
    dt_bias = dt0 + jnp.log(-jnp.expm1(-dt0))
    a_log = jnp.log(jax.random.uniform(ks[13], (DEPTH, N_SSM_HEADS), f32, 1.0, 16.0))
    d_skip = 1.0 + nrm(ks[14], (DEPTH, N_SSM_HEADS), 0.02)
    ssm_norm_w = 1.0 + nrm(ks[15], (DEPTH, D_INNER), 0.02)
    f_bias = jax.random.uniform(ks[16], (DEPTH, N_ATT_HEADS), f32, 4.0, 9.0)
    gate_bias = nrm(ks[17], (DEPTH, 2 * D_MODEL), 0.02)
    w_ssm_o = nrm(ks[18], (DEPTH, D_INNER, D_MODEL), DN_BETA * D_INNER ** -0.5)
    w_att_o = nrm(ks[19], (DEPTH, D_ATT, D_MODEL), DN_BETA * D_ATT ** -0.5)
    w_o = nrm(ks[20], (DEPTH, D_MODEL, D_MODEL), DN_BETA * D_MODEL ** -0.5)
    ln1_g = 1.0 + nrm(ks[21], (DEPTH, D_MODEL), 0.02)
    ln1_b = nrm(ks[22], (DEPTH, D_MODEL), 0.02)
    w_ffn_up = nrm(ks[23], (DEPTH, D_MODEL, 2 * D_FF), D_MODEL ** -0.5)
    ffn_conv_w = nrm(ks[24], (DEPTH, FFN_CONV, D_FF), FFN_CONV ** -0.5)
    ffn_conv_b = nrm(ks[25], (DEPTH, D_FF), 0.02)
    w_ffn_down = nrm(ks[26], (DEPTH, D_FF, D_MODEL), DN_BETA * D_FF ** -0.5)
    ln2_g = 1.0 + nrm(ks[27], (DEPTH, D_MODEL), 0.02)
    ln2_b = nrm(ks[28], (DEPTH, D_MODEL), 0.02)
    return {'x_prompt': x_prompt, 'x_sample': x_sample, 'cache_k': cache_k, 'cache_v': cache_v, 'cache_logf': cache_logf, 'state_conv_ssm': state_conv_ssm, 'state_ssm': state_ssm, 'state_conv_ffn': state_conv_ffn, 'page_table': page_table, 'w_in': w_in, 'conv_ssm_w': conv_ssm_w, 'conv_ssm_b': conv_ssm_b, 'dt_bias': dt_bias, 'a_log': a_log, 'd_skip': d_skip, 'ssm_norm_w': ssm_norm_w, 'f_bias': f_bias, 'gate_bias': gate_bias, 'w_ssm_o': w_ssm_o, 'w_att_o': w_att_o, 'w_o': w_o, 'ln1_g': ln1_g, 'ln1_b': ln1_b, 'w_ffn_up': w_ffn_up, 'ffn_conv_w': ffn_conv_w, 'ffn_conv_b': ffn_conv_b, 'w_ffn_down': w_ffn_down, 'ln2_g': ln2_g, 'ln2_b': ln2_b}


def reference(x_prompt, x_sample, cache_k, cache_v, cache_logf, state_conv_ssm, state_ssm, state_conv_ffn, page_table, w_in, conv_ssm_w, conv_ssm_b, dt_bias, a_log, d_skip, ssm_norm_w, f_bias, gate_bias, w_ssm_o, w_att_o, w_o, ln1_g, ln1_b, w_ffn_up, ffn_conv_w, ffn_conv_b, w_ffn_down, ln2_g, ln2_b):
    n_p = x_prompt.shape[0]
    n_s, n_pages = page_table.shape
    past_len = n_pages * cache_k.shape[2]
    h_p, h_s = x_prompt, x_sample
    new_p, new_s = [], []
    for l in range(DEPTH):
        lp = (w_in[l], conv_ssm_w[l], conv_ssm_b[l], dt_bias[l], a_log[l], d_skip[l], ssm_norm_w[l], f_bias[l], gate_bias[l], w_ssm_o[l], w_att_o[l], w_o[l], ln1_g[l], ln1_b[l], w_ffn_up[l], ffn_conv_w[l], ffn_conv_b[l], w_ffn_down[l], ln2_g[l], ln2_b[l])
        dtp = h_p.dtype
        h_p, st_p = _layer(h_p, jnp.zeros((n_p, SSM_CONV - 1, CONV_DIM), dtp), jnp.zeros((n_p, N_SSM_HEADS, SSM_HEAD_DIM, D_STATE), dtp), jnp.zeros((n_p, FFN_CONV - 1, D_FF), dtp), None, *lp)
        new_p.append(st_p)
        k_past = cache_k[l][page_table].reshape(n_s, past_len, N_ATT_HEADS, ATT_HEAD_DIM)
        v_past = cache_v[l][page_table].reshape(n_s, past_len, N_ATT_HEADS, ATT_HEAD_DIM)
        lf_past = cache_logf[l][page_table].reshape(n_s, past_len, N_ATT_HEADS)
        h_s, st_s = _layer(h_s, state_conv_ssm[l], state_ssm[l], state_conv_ffn[l], (k_past, v_past, lf_past), *lp)
        new_s.append(st_s)
    k_p, v_p, lf_p, cs_p, ss_p, cf_p = [jnp.stack(a) for a in zip(*new_p)]
    k_s, v_s, lf_s, cs_s, ss_s, cf_s = [jnp.stack(a) for a in zip(*new_s)]
    return (h_p, h_s, k_p, v_p, lf_p, cs_p, ss_p, cf_p, k_s, v_s, lf_s, cs_s, ss_s, cf_s)
```

```python
import functools
import math

import jax
import jax.numpy as jnp
from jax import lax
from jax.experimental import pallas as pl
from jax.experimental.pallas import tpu as pltpu

F32 = jnp.float32
BF16 = jnp.bfloat16

LN_EPS = 1e-5
RMS_EPS = 1e-5
SSD_CHUNK = 128
N_GROUPS = 2
NEG_BIG = -1e30
V7X_VMEM_BYTES = 64 * 1024 * 1024
VMEM_LIMIT = V7X_VMEM_BYTES - 8 * 1024 * 1024
LANES = 128
SUBLANES = 8
NEW_KEY_SLOTS = 16


def _dot(a, b):
    return jnp.dot(a, b, preferred_element_type=F32)


def _dot_nt(a, b):
    return lax.dot_general(a, b, (((1,), (1,)), ((), ())), preferred_element_type=F32)


def _split3(a):
    hi = a.astype(BF16)
    r1 = a - hi.astype(F32)
    mid = r1.astype(BF16)
    lo = (r1 - mid.astype(F32)).astype(BF16)
    return hi, mid, lo


def _dot3_l(a, m):
    hi, mid, lo = _split3(a)
    return _dot(hi, m) + _dot(mid, m) + _dot(lo, m)


def _dot3_r(m, a):
    hi, mid, lo = _split3(a)
    return _dot(m, hi) + _dot(m, mid) + _dot(m, lo)


def _silu(x):
    return x * (1.0 / (1.0 + jnp.exp(-x)))


def _sigmoid(x):
    return 1.0 / (1.0 + jnp.exp(-x))


def _layer_norm(x, g, b):
    mu = jnp.mean(x, axis=-1, keepdims=True)
    xc = x - mu
    var = jnp.mean(xc * xc, axis=-1, keepdims=True)
    return xc * lax.rsqrt(var + LN_EPS) * g + b


def _const_spec(shape):
    nd = len(shape)
    return pl.BlockSpec(shape, lambda *_: (0,) * nd, pipeline_mode=pl.Buffered(1))


def _params(sem):
    return pltpu.CompilerParams(dimension_semantics=sem, vmem_limit_bytes=VMEM_LIMIT)


def _in_proj_kernel(n_dt, x_ref, wz, wxbc, wq, wk, wv, wg, wsm, bsm,
                    z_o, xbc_o, q_o, kf_o, kb_o, vf_o, vb_o, g_o, dtf_o):
    xb = x_ref[...].astype(BF16)
    z_o[...] = _dot(xb, wz[...]).astype(BF16)
    xbc_o[...] = _dot(xb, wxbc[...])
    q_o[...] = _dot(xb, wq[...]).astype(BF16)
    k = _dot(xb, wk[...])
    kf_o[...] = k
    kb_o[...] = k.astype(BF16)
    v = _dot(xb, wv[...])
    vf_o[...] = v
    vb_o[...] = v.astype(BF16)
    g_o[...] = _dot(xb, wg[...]).astype(BF16)
    s = bsm[...] + _dot(xb, wsm[...])
    lane = lax.broadcasted_iota(jnp.int32, s.shape, 1)
    t = jnp.log1p(jnp.exp(-jnp.abs(s)))
    dtf_o[...] = jnp.where(lane < n_dt, jnp.maximum(s, 0.0) + t, jnp.minimum(s, 0.0) - t)


def _in_proj(x2d, w, tm):
    m, d = x2d.shape
    tm = min(tm, m)
    widths = [w[k].shape[1] for k in ("z", "xbc", "q", "k", "v", "g", "sm")]
    row = lambda i: (i, 0)
    in_specs = [pl.BlockSpec((tm, d), row)] + [_const_spec((d, wd)) for wd in widths] + [_const_spec((1, LANES))]
    outs = [("z", BF16), ("xbc", F32), ("q", BF16), ("k", F32), ("k", BF16), ("v", F32), ("v", BF16), ("g", BF16), ("sm", F32)]
    out_shape = [jax.ShapeDtypeStruct((m, w[k].shape[1]), dt) for k, dt in outs]
    out_specs = [pl.BlockSpec((tm, w[k].shape[1]), row) for k, _ in outs]
    return pl.pallas_call(
        functools.partial(_in_proj_kernel, w["n_dt"]),
        grid=(m // tm,), in_specs=in_specs, out_specs=out_specs, out_shape=out_shape,
        compiler_params=_params(("parallel",)), name="in_proj",
    )(x2d, w["z"], w["xbc"], w["q"], w["k"], w["v"], w["g"], w["sm"], w["bsm"])


def _ssd_prompt_kernel(n_h, p_dim, d_state, n_att,
                       xbc_ref, dtf_ref, z_ref, cw_ref, cb_ref, alog_ref, dskip_ref, nw_ref, e_ref, tri_ref,
                       y_o, negc_o, hout_o, ext, ht, carry, ysc):
    c = pl.program_id(1)
    nc = pl.num_programs(1)
    T = xbc_ref.shape[1]
    d_inner = n_h * p_dim
    gw = N_GROUPS * d_state
    hpg = n_h // N_GROUPS

    @pl.when(c == 0)
    def _():
        ext[0:SUBLANES, :] = jnp.zeros((SUBLANES, ext.shape[1]), F32)
        ht[...] = jnp.zeros(ht.shape, F32)
        carry[...] = jnp.zeros(carry.shape, F32)

    ext[SUBLANES:SUBLANES + T, :] = xbc_ref[0]
    width = cw_ref.shape[0]
    conv = cb_ref[...] + ext[SUBLANES:SUBLANES + T, :] * cw_ref[width - 1:width, :]
    for j in range(1, width):
        conv = conv + ext[SUBLANES - j:SUBLANES - j + T, :] * cw_ref[width - 1 - j:width - j, :]
    ext[0:SUBLANES, :] = ext[T:T + SUBLANES, :]
    act = _silu(conv)
    xs = act[:, :d_inner]
    bm = act[:, d_inner:d_inner + gw]
    cm = act[:, d_inner + gw:d_inner + 2 * gw]

    dtf = dtf_ref[0]
    lane = lax.broadcasted_iota(jnp.int32, dtf.shape, 1)
    a_row = -jnp.exp(alog_ref[...])
    dt = jnp.where(lane < n_h, dtf, 0.0)
    comb = jnp.where(lane < n_h, dtf * a_row, jnp.where(lane < n_h + n_att, dtf, 0.0))
    cs = _dot3_r(tri_ref[...], comb)
    cs_t = cs.T
    acs = jnp.where(lane < n_h, cs, 0.0)

    clf = cs_t[n_h:n_h + n_att, :] + carry[...][:, 0:1]
    negc_o[0] = -clf
    carry[...] = jnp.broadcast_to(clf[:, T - 1:T], carry.shape)

    e = e_ref[...]
    dt_e = _dot3_l(dt, e)
    acs_e = _dot3_l(acs, e)
    acs_last = acs_e[T - 1:T, :]
    xdt = xs * dt_e

    row_i = lax.broadcasted_iota(jnp.int32, (T, T), 0)
    col_i = lax.broadcasted_iota(jnp.int32, (T, T), 1)
    causal = col_i <= row_i
    half = lax.broadcasted_iota(jnp.int32, (T, 2 * p_dim), 1) < p_dim
    for g in range(N_GROUPS):
        cg = cm[:, g * d_state:(g + 1) * d_state].astype(BF16)
        bg = bm[:, g * d_state:(g + 1) * d_state].astype(BF16)
        cbm = _dot_nt(cg, bg)
        for pr in range(hpg // 2):
            h0 = g * hpg + 2 * pr
            ms = []
            for h in (h0, h0 + 1):
                seg = cs[:, h:h + 1] - cs_t[h:h + 1, :]
                dec = jnp.exp(jnp.where(causal, seg, NEG_BIG))
                ms.append((cbm * dec).astype(BF16))
            lo = h0 * p_dim
            xp = xdt[:, lo:lo + 2 * p_dim].astype(BF16)
            ysc[:, lo:lo + 2 * p_dim] = jnp.where(half, _dot(ms[0], xp), _dot(ms[1], xp))

    ht_old = ht[...]
    ht_b = ht_old.astype(BF16)
    xw = (xdt * jnp.exp(acs_last - acs_e)).astype(BF16)
    gl = hpg * p_dim
    y_off = []
    s_new = []
    for g in range(N_GROUPS):
        cg = cm[:, g * d_state:(g + 1) * d_state].astype(BF16)
        y_off.append(_dot(cg, ht_b[:, g * gl:(g + 1) * gl]))
        bgt = bm[:, g * d_state:(g + 1) * d_state].T.astype(BF16)
        s_new.append(_dot(bgt, xw[:, g * gl:(g + 1) * gl]))
    y_off = jnp.concatenate(y_off, axis=1) * jnp.exp(acs_e)
    ht_new = ht_old * jnp.exp(acs_last) + jnp.concatenate(s_new, axis=1)
    ht[...] = ht_new

    y = ysc[...] + y_off + xs * dskip_ref[...]
    y = y * _silu(z_ref[0].astype(F32))
    outs = []
    for g in range(N_GROUPS):
        yg = y[:, g * gl:(g + 1) * gl]
        outs.append(yg * lax.rsqrt(jnp.mean(yg * yg, axis=-1, keepdims=True) + RMS_EPS))
    y_o[0] = (jnp.concatenate(outs, axis=1) * nw_ref[...]).astype(y_o.dtype)

    @pl.when(c == nc - 1)
    def _():
        hout_o[0] = ht_new.T


def _ssd_prompt(xbc, dtf, z, p, n_h, p_dim, d_state, n_att):
    n, L, cdim = xbc.shape
    T = SSD_CHUNK if L % SSD_CHUNK == 0 else L
    d_inner = n_h * p_dim
    tri = jnp.tril(jnp.ones((T, T), F32)).astype(BF16)
    blk = lambda b, c: (b, c, 0)
    in_specs = [pl.BlockSpec((1, T, cdim), blk), pl.BlockSpec((1, T, LANES), blk), pl.BlockSpec((1, T, d_inner), blk),
                _const_spec(p["conv_w"].shape), _const_spec((1, cdim)), _const_spec((1, LANES)),
                _const_spec((1, d_inner)), _const_spec((1, d_inner)), _const_spec((LANES, d_inner)), _const_spec((T, T))]
    out_shape = [jax.ShapeDtypeStruct((n, L, d_inner), BF16), jax.ShapeDtypeStruct((n, n_att, L), F32),
                 jax.ShapeDtypeStruct((n, d_inner, d_state), F32)]
    out_specs = [pl.BlockSpec((1, T, d_inner), blk), pl.BlockSpec((1, n_att, T), lambda b, c: (b, 0, c)),
                 pl.BlockSpec((1, d_inner, d_state), lambda b, c: (b, 0, 0))]
    scratch = [pltpu.VMEM((T + SUBLANES, cdim), F32), pltpu.VMEM((d_state, d_inner), F32),
               pltpu.VMEM((n_att, LANES), F32), pltpu.VMEM((T, d_inner), F32)]
    return pl.pallas_call(
        functools.partial(_ssd_prompt_kernel, n_h, p_dim, d_state, n_att),
        grid=(n, L // T), in_specs=in_specs, out_specs=out_specs, out_shape=out_shape, scratch_shapes=scratch,
        compiler_params=_params(("parallel", "arbitrary")), name="ssd_prompt",
    )(xbc, dtf, z, p["conv_w"], p["conv_b"], p["a_log"], p["d_skip_e"], p["norm_w"], p["e_heads"], tri)


def _fox_prompt_kernel(hd, q_ref, k_ref, v_ref, nc_ref, o_ref, m_s, l_s, acc_s):
    i = pl.program_id(2)
    j = pl.program_id(3)
    tq = q_ref.shape[0]
    tk = k_ref.shape[0]
    lane_q = lax.broadcasted_iota(jnp.int32, (tq, 2 * hd), 1)

    @pl.when(j == 0)
    def _():
        m_s[...] = jnp.full(m_s.shape, NEG_BIG, F32)
        l_s[...] = jnp.zeros(l_s.shape, F32)
        acc_s[...] = jnp.zeros(acc_s.shape, F32)

    def step(masked):
        q = q_ref[...]
        k = k_ref[...]
        v = v_ref[...]
        for hh in range(2):
            qm = jnp.where((lane_q >= hh * hd) & (lane_q < (hh + 1) * hd), q, jnp.zeros_like(q))
            u = nc_ref[hh:hh + 1, :] + _dot_nt(qm, k)
            if masked:
                row_i = lax.broadcasted_iota(jnp.int32, (tq, tk), 0)
                col_i = lax.broadcasted_iota(jnp.int32, (tq, tk), 1)
                u = jnp.where(col_i <= row_i, u, NEG_BIG)
            m_old = m_s[hh]
            m_new = jnp.maximum(m_old, jnp.max(u, axis=-1, keepdims=True))
            alpha = jnp.exp(m_old - m_new)
            pm = jnp.exp(u - m_new)
            l_s[hh] = alpha * l_s[hh] + jnp.sum(pm, axis=-1, keepdims=True)
            acc_s[hh] = alpha * acc_s[hh] + _dot(pm.astype(BF16), v)
            m_s[hh] = m_new

    @pl.when(j < i)
    def _():
        step(False)

    @pl.when(j == i)
    def _():
        step(True)
        o0 = acc_s[0] / l_s[0]
        o1 = acc_s[1] / l_s[1]
        o_ref[...] = jnp.where(lane_q < hd, o0, o1).astype(o_ref.dtype)


def _fox_prompt(q, k, v, negc, hd, blk):
    n, L, da = q.shape
    n_pairs = da // (2 * hd)
    t = min(blk, L)
    nb = L // t
    negc4 = negc.reshape(n, n_pairs, 2, L)
    in_specs = [pl.BlockSpec((None, t, 2 * hd), lambda b, p, i, j: (b, i, p)),
                pl.BlockSpec((None, t, 2 * hd), lambda b, p, i, j: (b, jnp.minimum(j, i), p)),
                pl.BlockSpec((None, t, 2 * hd), lambda b, p, i, j: (b, jnp.minimum(j, i), p)),
                pl.BlockSpec((None, None, 2, t), lambda b, p, i, j: (b, p, 0, jnp.minimum(j, i)))]
    out_specs = pl.BlockSpec((None, t, 2 * hd), lambda b, p, i, j: (b, i, p))
    scratch = [pltpu.VMEM((2, t, 1), F32), pltpu.VMEM((2, t, 1), F32), pltpu.VMEM((2, t, 2 * hd), F32)]
    return pl.pallas_call(
        functools.partial(_fox_prompt_kernel, hd),
        grid=(n, n_pairs, nb, nb), in_specs=in_specs, out_specs=out_specs,
        out_shape=jax.ShapeDtypeStruct((n, L, da), BF16), scratch_shapes=scratch,
        compiler_params=_params(("parallel", "parallel", "parallel", "arbitrary")), name="fox_prompt",
    )(q, k, v, negc4)


def _merge_kernel(alpha, x_ref, ys_ref, att_ref, g_ref, wso, wao, wo, gb, lg, lb, h_o):
    d = x_ref.shape[1]
    a = _dot(ys_ref[...], wso[...])
    b = _dot(att_ref[...], wao[...])
    gt = _sigmoid(g_ref[...].astype(F32) + gb[...])
    merged = gt[:, :d] * a + gt[:, d:] * b
    o = _dot(merged.astype(BF16), wo[...])
    h_o[...] = _layer_norm(alpha * x_ref[...] + o, lg[...], lb[...])


def _merge(x2d, ys, att, gates, p, alpha, tm):
    m, d = x2d.shape
    tm = min(tm, m)
    row = lambda i: (i, 0)
    ins = [x2d, ys, att, gates]
    consts = [p["w_ssm_o"], p["w_att_o"], p["w_o"], p["gate_bias"], p["ln1_g"], p["ln1_b"]]
    in_specs = [pl.BlockSpec((tm, a.shape[1]), row) for a in ins] + [_const_spec(c.shape) for c in consts]
    return pl.pallas_call(
        functools.partial(_merge_kernel, alpha),
        grid=(m // tm,), in_specs=in_specs, out_specs=pl.BlockSpec((tm, d), row),
        out_shape=jax.ShapeDtypeStruct((m, d), F32),
        compiler_params=_params(("parallel",)), name="merge",
    )(*ins, *consts)


def _ffn_kernel(alpha, shift, n_col_chunks, h_ref, prev_ref, wup, cw, cb, wdn, lg, lb, o_ref, tail_o, ext, act):
    c = pl.program_id(1)
    tm = h_ref.shape[0] * h_ref.shape[1]
    dff = cw.shape[1]
    width = cw.shape[0]
    r = ext.shape[0] - tm
    cwid = dff // n_col_chunks

    @pl.when(c == 0)
    def _():
        ext[0:r, :] = prev_ref[...].reshape(r, dff)

    h = h_ref[...].reshape(tm, h_ref.shape[2])
    hb = h.astype(BF16)
    for ch in range(n_col_chunks):
        lo = ch * cwid
        ext[r:r + tm, lo:lo + cwid] = _dot(hb, wup[:, lo:lo + cwid])
        val = _dot(hb, wup[:, dff + lo:dff + lo + cwid])
        conv = cb[:, lo:lo + cwid] + ext[r:r + tm, lo:lo + cwid] * cw[width - 1:width, lo:lo + cwid]
        for j in range(1, width):
            conv = conv + ext[r - j * shift:r - j * shift + tm, lo:lo + cwid] * cw[width - 1 - j:width - j, lo:lo + cwid]
        gelu = 0.5 * conv * (1.0 + lax.erf(conv * math.sqrt(0.5)))
        act[:, lo:lo + cwid] = (gelu * val).astype(BF16)
    tail = ext[tm:tm + r, :]
    ext[0:r, :] = tail
    tail_o[...] = tail.reshape(tail_o.shape)
    o_ref[...] = _layer_norm(alpha * h + _dot(act[...], wdn[...]), lg[...], lb[...]).reshape(o_ref.shape)


def _ffn(h3, prev, p, alpha, time_major, tile):
    d = h3.shape[2]
    dff = p["ffn_conv_w"].shape[1]
    if time_major:
        nt, n_s, _ = h3.shape
        tile = min(tile, n_s)
        hblk, pblk = (nt, tile, d), (prev.shape[0], tile, dff)
        hmap = pmap = lambda b, c: (0, b, 0)
        grid, shift = (n_s // tile, 1), tile
    else:
        n, L, _ = h3.shape
        tile = min(tile, L)
        hblk, pblk = (1, tile, d), (1, prev.shape[1], dff)
        hmap, pmap = (lambda b, c: (b, c, 0)), (lambda b, c: (b, 0, 0))
        grid, shift = (n, L // tile), 1
    tm, r = hblk[0] * hblk[1], pblk[0] * pblk[1]
    consts = [p["w_ffn_up"], p["ffn_conv_w"], p["ffn_conv_b"], p["w_ffn_down"], p["ln2_g"], p["ln2_b"]]
    in_specs = [pl.BlockSpec(hblk, hmap), pl.BlockSpec(pblk, pmap)] + [_const_spec(cst.shape) for cst in consts]
    out_shape = [jax.ShapeDtypeStruct(h3.shape, F32), jax.ShapeDtypeStruct(prev.shape, F32)]
    out_specs = [pl.BlockSpec(hblk, hmap), pl.BlockSpec(pblk, pmap)]
    scratch = [pltpu.VMEM((tm + r, dff), F32), pltpu.VMEM((tm, dff), BF16)]
    return pl.pallas_call(
        functools.partial(_ffn_kernel, alpha, shift, 2),
        grid=grid, in_specs=in_specs, out_specs=out_specs, out_shape=out_shape, scratch_shapes=scratch,
        compiler_params=_params(("parallel", "arbitrary")), name="ffn",
    )(h3, prev, *consts)


def _ssd_sample_kernel(n_h, p_dim, d_state,
                       xbc_ref, prev_ref, dtf_ref, z_ref, st_ref, cw_ref, cb_ref, alog_ref, dskip_ref, nw_ref, e_ref,
                       y_o, st_o, xw_s, b_s, c_s, dec_s, yoff_s):
    nt, sb, cdim = xbc_ref.shape
    width = cw_ref.shape[0]
    d_inner = n_h * p_dim
    gw = N_GROUPS * d_state
    hpg = n_h // N_GROUPS
    gl = hpg * p_dim
    rows = nt * sb

    a_row = -jnp.exp(alog_ref[...])
    xin = [prev_ref[j] for j in range(width - 1)] + [xbc_ref[t] for t in range(nt)]
    xs, bm, cm, dts, acs = [], [], [], [], []
    run = None
    for t in range(nt):
        conv = cb_ref[...]
        for j in range(width):
            conv = conv + xin[t + j] * cw_ref[j:j + 1, :]
        act = _silu(conv)
        xs.append(act[:, :d_inner])
        bm.append(act[:, d_inner:d_inner + gw])
        cm.append(act[:, d_inner + gw:d_inner + 2 * gw])
        dtf = dtf_ref[t]
        lane = lax.broadcasted_iota(jnp.int32, dtf.shape, 1)
        dt = jnp.where(lane < n_h, dtf, 0.0)
        run = dt * a_row if run is None else run + dt * a_row
        dts.append(dt)
        acs.append(run)
    stacked = jnp.concatenate(dts + acs, axis=0)
    exp_all = _dot3_l(stacked, e_ref[...])
    dt_e = [exp_all[t * sb:(t + 1) * sb] for t in range(nt)]
    acs_e = [exp_all[(nt + t) * sb:(nt + t + 1) * sb] for t in range(nt)]
    xdt = [xs[t] * dt_e[t] for t in range(nt)]

    cbf = [cm[t].astype(BF16).astype(F32) for t in range(nt)]
    bbf = [bm[t].astype(BF16).astype(F32) for t in range(nt)]
    y_diag = []
    for t in range(nt):
        acc = jnp.zeros((sb, d_inner), F32)
        for s in range(t + 1):
            parts = []
            for g in range(N_GROUPS):
                dotg = jnp.sum(cbf[t][:, g * d_state:(g + 1) * d_state] * bbf[s][:, g * d_state:(g + 1) * d_state],
                               axis=-1, keepdims=True)
                w = jnp.exp(acs_e[t][:, g * gl:(g + 1) * gl] - acs_e[s][:, g * gl:(g + 1) * gl])
                parts.append(dotg * w * xdt[s][:, g * gl:(g + 1) * gl])
            acc = acc + jnp.concatenate(parts, axis=1)
        y_diag.append(acc)

    pad = xw_s.shape[0] - rows
    for t in range(nt):
        xw_s[t * sb:(t + 1) * sb, :] = xdt[t] * jnp.exp(acs_e[nt - 1] - acs_e[t])
        b_s[t * sb:(t + 1) * sb, :] = bm[t]
        c_s[t * sb:(t + 1) * sb, :] = cm[t]
    if pad:
        xw_s[rows:, :] = jnp.zeros((pad, d_inner), F32)
        b_s[rows:, :] = jnp.zeros((pad, gw), F32)
        c_s[rows:, :] = jnp.zeros((pad, gw), F32)
    dec_s[...] = jnp.exp(acs[nt - 1])
    yoff_s[...] = jnp.zeros(yoff_s.shape, F32)
    xw_t = xw_s[...].T.astype(BF16)
    rp = xw_s.shape[0]
    row_id = lax.broadcasted_iota(jnp.int32, (rp, 1), 0)

    def per_seq(j, carry):
        sel = (row_id % sb) == j
        h0 = st_ref[j]
        h0b = h0.astype(BF16)
        dec = dec_s[pl.ds(j, 1), :]
        bsel = jnp.where(sel, b_s[...], 0.0).astype(BF16)
        csel = jnp.where(sel, c_s[...], 0.0).astype(BF16)
        for g in range(N_GROUPS):
            yo = _dot_nt(csel[:, g * d_state:(g + 1) * d_state], h0b[g * gl:(g + 1) * gl, :])
            yoff_s[:, g * gl:(g + 1) * gl] = yoff_s[:, g * gl:(g + 1) * gl] + yo
            upd = _dot(xw_t[g * gl:(g + 1) * gl, :], bsel[:, g * d_state:(g + 1) * d_state])
            for r in range(hpg):
                h = g * hpg + r
                lo = h * p_dim
                dcol = jnp.broadcast_to(dec[:, h:h + 1], (p_dim, d_state))
                st_o[j, lo:lo + p_dim, :] = h0[lo:lo + p_dim, :] * dcol + upd[r * p_dim:(r + 1) * p_dim, :]
        return carry

    lax.fori_loop(0, sb, per_seq, 0)

    for t in range(nt):
        y = y_diag[t] + yoff_s[t * sb:(t + 1) * sb, :] * jnp.exp(acs_e[t]) + xs[t] * dskip_ref[...]
        y = y * _silu(z_ref[t].astype(F32))
        outs = []
        for g in range(N_GROUPS):
            yg = y[:, g * gl:(g + 1) * gl]
            outs.append(yg * lax.rsqrt(jnp.mean(yg * yg, axis=-1, keepdims=True) + RMS_EPS))
        y_o[t] = (jnp.concatenate(outs, axis=1) * nw_ref[...]).astype(y_o.dtype)


def _ssd_sample(xbc_t, prev_t, dtf_t, z_t, state, p, n_h, p_dim, d_state, sb):
    nt, s, cdim = xbc_t.shape
    sb = min(sb, s)
    d_inner = n_h * p_dim
    gw = N_GROUPS * d_state
    rows_pad = max(LANES, -(-nt * sb // LANES) * LANES)
    tb = lambda i: (0, i, 0)
    in_specs = [pl.BlockSpec((nt, sb, cdim), tb), pl.BlockSpec((prev_t.shape[0], sb, cdim), tb),
                pl.BlockSpec((nt, sb, LANES), tb), pl.BlockSpec((nt, sb, d_inner), tb),
                pl.BlockSpec((sb, d_inner, d_state), lambda i: (i, 0, 0)),
                _const_spec(p["conv_w"].shape), _const_spec((1, cdim)), _const_spec((1, LANES)),
                _const_spec((1, d_inner)), _const_spec((1, d_inner)), _const_spec((LANES, d_inner))]
    out_shape = [jax.ShapeDtypeStruct((nt, s, d_inner), BF16), jax.ShapeDtypeStruct(state.shape, F32)]
    out_specs = [pl.BlockSpec((nt, sb, d_inner), tb), pl.BlockSpec((sb, d_inner, d_state), lambda i: (i, 0, 0))]
    scratch = [pltpu.VMEM((rows_pad, d_inner), F32), pltpu.VMEM((rows_pad, gw), F32), pltpu.VMEM((rows_pad, gw), F32),
               pltpu.VMEM((sb, LANES), F32), pltpu.VMEM((rows_pad, d_inner), F32)]
    return pl.pallas_call(
        functools.partial(_ssd_sample_kernel, n_h, p_dim, d_state),
        grid=(s // sb,), in_specs=in_specs, out_specs=out_specs, out_shape=out_shape, scratch_shapes=scratch,
        compiler_params=_params(("parallel",)), name="ssd_sample",
    )(xbc_t, prev_t, dtf_t, z_t, state, p["conv_w"], p["conv_b"], p["a_log"], p["d_skip_e"], p["norm_w"], p["e_heads"])


def _fox_sample_kernel(n_pages, ch, page, n_heads, hd, nt,
                       pt_ref, qbd_ref, kn_ref, vn_ref, lfn_ref, su_ref, k_hbm, v_hbm, lf_hbm,
                       o_ref, kbuf, vbuf, lfbuf, ksem, vsem, lfsem):
    s = pl.program_id(0)
    ns = pl.num_programs(0)
    n_chunks = n_pages // ch
    rows = nt * n_heads
    da = n_heads * hd

    def kv_copies(seq, chunk, slot):
        cps = []
        for i in range(ch):
            pid = pt_ref[seq, chunk * ch + i]
            cps.append(pltpu.make_async_copy(k_hbm.at[pid], kbuf.at[slot, pl.ds(i * page, page)], ksem.at[slot]))
            cps.append(pltpu.make_async_copy(v_hbm.at[pid], vbuf.at[slot, pl.ds(i * page, page)], vsem.at[slot]))
        return cps

    def lf_copies(seq, slot):
        return [pltpu.make_async_copy(lf_hbm.at[pt_ref[seq, i]], lfbuf.at[slot, i], lfsem.at[slot]) for i in range(n_pages)]

    @pl.when(s == 0)
    def _():
        for cp in lf_copies(0, 0):
            cp.start()
        for cp in kv_copies(0, 0, 0):
            cp.start()

    ls = s % 2
    for cp in lf_copies(s, ls):
        cp.wait()

    @pl.when(s + 1 < ns)
    def _():
        for cp in lf_copies(s + 1, 1 - ls):
            cp.start()

    lf = lfbuf[ls]
    lf2 = lf.reshape(n_pages * n_heads, page)
    within = _dot3_l(lf2, su_ref[...]).reshape(n_pages, n_heads, page)
    tot = jnp.sum(lf, axis=-1, keepdims=True)
    bias_pages = [None] * n_pages
    run = jnp.zeros((n_heads, 1), F32)
    for i in reversed(range(n_pages)):
        bias_pages[i] = within[i] + run
        run = run + tot[i]

    qbd = qbd_ref[0]
    m_run = jnp.full((rows, 1), NEG_BIG, F32)
    l_run = jnp.zeros((rows, 1), F32)
    acc = jnp.zeros((rows, da), F32)
    for c in range(n_chunks):
        g = s * n_chunks + c
        slot = g % 2
        if c + 1 < n_chunks:
            for cp in kv_copies(s, c + 1, 1 - slot):
                cp.start()
        else:
            @pl.when(s + 1 < ns)
            def _():
                for cp in kv_copies(s + 1, 0, 1 - slot):
                    cp.start()
        for cp in kv_copies(s, c, slot):
            cp.wait()
        kc = kbuf[slot].astype(BF16)
        vc = vbuf[slot].astype(BF16)
        bias = jnp.concatenate(bias_pages[c * ch:(c + 1) * ch], axis=1)
        u = jnp.concatenate([bias] * nt, axis=0) + _dot_nt(qbd, kc)
        m_new = jnp.maximum(m_run, jnp.max(u, axis=-1, keepdims=True))
        alpha = jnp.exp(m_run - m_new)
        pm = jnp.exp(u - m_new)
        l_run = alpha * l_run + jnp.sum(pm, axis=-1, keepdims=True)
        acc = alpha * acc + _dot(pm.astype(BF16), vc)
        m_run = m_new

    lfn = lfn_ref[0]
    lane = lax.broadcasted_iota(jnp.int32, lfn.shape, 1)
    cn = jnp.zeros(lfn.shape, F32)
    for t in range(nt):
        cn = cn + jnp.where(lane >= t, lfn[:, t:t + 1], 0.0)
    n_slots = kn_ref.shape[1]
    nbias = jnp.concatenate([-cn[:, :n_slots]] * nt, axis=0)
    row_t = lax.broadcasted_iota(jnp.int32, (rows, n_slots), 0) // n_heads
    lane_r = lax.broadcasted_iota(jnp.int32, (rows, n_slots), 1)
    u = jnp.where(lane_r <= row_t, nbias + _dot_nt(qbd, kn_ref[0]), NEG_BIG)
    m_new = jnp.maximum(m_run, jnp.max(u, axis=-1, keepdims=True))
    alpha = jnp.exp(m_run - m_new)
    pm = jnp.exp(u - m_new)
    l_run = alpha * l_run + jnp.sum(pm, axis=-1, keepdims=True)
    acc = alpha * acc + _dot(pm.astype(BF16), vn_ref[0])
    out = acc / l_run
    col_h = lax.broadcasted_iota(jnp.int32, (rows, da), 1) // hd
    row_h = lax.broadcasted_iota(jnp.int32, (rows, da), 0) % n_heads
    out = jnp.where(col_h == row_h, out, 0.0)
    o_ref[0] = jnp.concatenate(
        [jnp.sum(out[t * n_heads:(t + 1) * n_heads], axis=0, keepdims=True) for t in range(nt)]
        + [jnp.zeros((SUBLANES - nt, da), F32)], axis=0).astype(o_ref.dtype)


def _fox_sample(page_table, qbd, k_new, v_new, lf_new, cache_k, cache_v, cache_lft, n_heads, hd, nt, ch):
    s, n_pages = page_table.shape
    n_pool, page, da = cache_k.shape
    ch = min(ch, n_pages)
    while n_pages % ch:
        ch -= 1
    rows = nt * n_heads
    su = jnp.triu(jnp.ones((page, page), F32), k=1).T.astype(BF16)
    per_seq = lambda i, pt: (i, 0, 0)
    grid_spec = pltpu.PrefetchScalarGridSpec(
        num_scalar_prefetch=1, grid=(s,),
        in_specs=[pl.BlockSpec((1, rows, da), per_seq), pl.BlockSpec((1, k_new.shape[1], da), per_seq),
                  pl.BlockSpec((1, k_new.shape[1], da), per_seq), pl.BlockSpec((1, n_heads, LANES), per_seq),
                  pl.BlockSpec((page, page), lambda i, pt: (0, 0)),
                  pl.BlockSpec(memory_space=pl.ANY), pl.BlockSpec(memory_space=pl.ANY), pl.BlockSpec(memory_space=pl.ANY)],
        out_specs=pl.BlockSpec((1, SUBLANES, da), per_seq),
        scratch_shapes=[pltpu.VMEM((2, ch * page, da), F32), pltpu.VMEM((2, ch * page, da), F32),
                        pltpu.VMEM((2, n_pages, n_heads, page), F32),
                        pltpu.SemaphoreType.DMA((2,)), pltpu.SemaphoreType.DMA((2,)), pltpu.SemaphoreType.DMA((2,))])
    return pl.pallas_call(
        functools.partial(_fox_sample_kernel, n_pages, ch, page, n_heads, hd, nt),
        grid_spec=grid_spec, out_shape=jax.ShapeDtypeStruct((s, SUBLANES, da), BF16),
        compiler_params=_params(("arbitrary",)), name="fox_sample",
    )(page_table, qbd, k_new, v_new, lf_new, su, cache_k, cache_v, cache_lft)


def _prep_params(dims, w_in, conv_ssm_w, conv_ssm_b, dt_bias, a_log, d_skip, ssm_norm_w, f_bias, gate_bias,
                 w_ssm_o, w_att_o, w_o, ln1_g, ln1_b, w_ffn_up, ffn_conv_w, ffn_conv_b, w_ffn_down, ln2_g, ln2_b):
    d_inner, cdim, n_h, d_att, n_att, d_model, p_dim, hd = dims
    z_end = d_inner
    xbc_end = z_end + cdim
    dt_end = xbc_end + n_h
    q_end = dt_end + d_att
    k_end = q_end + d_att
    v_end = k_end + d_att
    f_end = v_end + n_att
    row = lambda a: a.reshape(1, -1).astype(F32)
    pad_lanes = lambda a: jnp.pad(a, ((0, 0), (0, LANES - a.shape[1])))
    scale = hd ** -0.5
    w = {
        "n_dt": n_h,
        "z": w_in[:, :z_end].astype(BF16),
        "xbc": w_in[:, z_end:xbc_end].astype(BF16),
        "q": (w_in[:, dt_end:q_end] * scale).astype(BF16),
        "k": w_in[:, q_end:k_end].astype(BF16),
        "v": w_in[:, k_end:v_end].astype(BF16),
        "g": w_in[:, f_end:].astype(BF16),
        "sm": pad_lanes(jnp.concatenate([w_in[:, xbc_end:dt_end], w_in[:, v_end:f_end]], axis=1)).astype(BF16),
        "bsm": pad_lanes(jnp.concatenate([row(dt_bias), row(f_bias)], axis=1)),
    }
    head_of_lane = jnp.arange(d_inner) // p_dim
    p = {
        "conv_w": conv_ssm_w.astype(F32), "conv_b": row(conv_ssm_b),
        "a_log": pad_lanes(row(a_log)),
        "d_skip_e": row(jnp.repeat(d_skip, p_dim)), "norm_w": row(ssm_norm_w),
        "e_heads": (jnp.arange(LANES)[:, None] == head_of_lane[None, :]).astype(BF16),
        "w_ssm_o": w_ssm_o.astype(BF16), "w_att_o": w_att_o.astype(BF16), "w_o": w_o.astype(BF16),
        "gate_bias": row(gate_bias), "ln1_g": row(ln1_g), "ln1_b": row(ln1_b),
        "w_ffn_up": w_ffn_up.astype(BF16), "ffn_conv_w": ffn_conv_w.astype(F32), "ffn_conv_b": row(ffn_conv_b),
        "w_ffn_down": w_ffn_down.astype(BF16), "ln2_g": row(ln2_g), "ln2_b": row(ln2_b),
    }
    return w, p


def _forward(x_prompt, x_sample, cache_k, cache_v, cache_logf, state_conv_ssm, state_ssm, state_conv_ffn, page_table,
             w_in, conv_ssm_w, conv_ssm_b, dt_bias, a_log, d_skip, ssm_norm_w, f_bias, gate_bias, w_ssm_o, w_att_o,
             w_o, ln1_g, ln1_b, w_ffn_up, ffn_conv_w, ffn_conv_b, w_ffn_down, ln2_g, ln2_b,
             tm_proj=256, tm_merge=512, tm_ffn=256, att_blk=512, sb=8, sbf=32, ch=16):
    depth = w_in.shape[0]
    assert depth == 1, "single-layer step"
    n_p, L, d_model = x_prompt.shape
    n_s, nt, _ = x_sample.shape
    n_pool, page, n_att, hd = cache_k.shape[1:]
    n_h, p_dim, d_state = state_ssm.shape[2:]
    d_inner = n_h * p_dim
    cdim = conv_ssm_w.shape[2]
    d_att = n_att * hd
    dff = ffn_conv_w.shape[2]
    ssm_w = conv_ssm_w.shape[1]
    ffn_w = ffn_conv_w.shape[1]
    alpha = (2.0 * depth) ** 0.25
    dims = (d_inner, cdim, n_h, d_att, n_att, d_model, p_dim, hd)
    lyr = 0
    w, p = _prep_params(dims, w_in[lyr], conv_ssm_w[lyr], conv_ssm_b[lyr], dt_bias[lyr], a_log[lyr], d_skip[lyr],
                        ssm_norm_w[lyr], f_bias[lyr], gate_bias[lyr], w_ssm_o[lyr], w_att_o[lyr], w_o[lyr], ln1_g[lyr],
                        ln1_b[lyr], w_ffn_up[lyr], ffn_conv_w[lyr], ffn_conv_b[lyr], w_ffn_down[lyr], ln2_g[lyr], ln2_b[lyr])

    xp2 = x_prompt.reshape(n_p * L, d_model)
    z, xbc, q, kf, kb, vf, vb, gates, dtf = _in_proj(xp2, w, tm_proj)
    xbc3 = xbc.reshape(n_p, L, cdim)
    y_ssm, negc, h_fin = _ssd_prompt(xbc3, dtf.reshape(n_p, L, LANES), z.reshape(n_p, L, d_inner), p, n_h, p_dim, d_state, n_att)
    att = _fox_prompt(q.reshape(n_p, L, d_att), kb.reshape(n_p, L, d_att), vb.reshape(n_p, L, d_att), negc, hd, att_blk)
    h1 = _merge(xp2, y_ssm.reshape(n_p * L, d_inner), att.reshape(n_p * L, d_att), gates, p, alpha, tm_merge)
    prev0 = jnp.zeros((n_p, SUBLANES, dff), F32)
    y_p, tail_p = _ffn(h1.reshape(n_p, L, d_model), prev0, p, alpha, False, tm_ffn)
    k_p = kf.reshape(1, n_p, L, n_att, hd)
    v_p = vf.reshape(1, n_p, L, n_att, hd)
    lf_p = dtf[:, n_h:n_h + n_att].reshape(1, n_p, L, n_att)
    cs_p = xbc3[:, L - (ssm_w - 1):, :][None]
    ss_p = h_fin.reshape(1, n_p, n_h, p_dim, d_state)
    cf_p = tail_p[:, SUBLANES - (ffn_w - 1):, :][None]

    xs2 = jnp.transpose(x_sample, (1, 0, 2)).reshape(nt * n_s, d_model)
    z, xbc, q, kf, kb, vf, vb, gates, dtf = _in_proj(xs2, w, tm_proj)
    xbc_t = xbc.reshape(nt, n_s, cdim)
    prev_t = jnp.transpose(state_conv_ssm[lyr], (1, 0, 2))
    y_ssm, ss_new = _ssd_sample(xbc_t, prev_t, dtf.reshape(nt, n_s, LANES), z.reshape(nt, n_s, d_inner),
                                state_ssm[lyr].reshape(n_s, d_inner, d_state), p, n_h, p_dim, d_state, sb)
    q_t = q.reshape(nt, n_s, n_att, hd)
    eye = jnp.eye(n_att, dtype=BF16)
    qbd = jnp.einsum("tshe,hg->sthge", q_t, eye).reshape(n_s, nt * n_att, d_att)
    seq_major = lambda a: jnp.pad(jnp.transpose(a.reshape(nt, n_s, d_att), (1, 0, 2)), ((0, 0), (0, NEW_KEY_SLOTS - nt), (0, 0)))
    lf_new = jnp.transpose(dtf[:, n_h:n_h + n_att].reshape(nt, n_s, n_att), (1, 2, 0))
    lf_new = jnp.pad(lf_new, ((0, 0), (0, 0), (0, LANES - nt)))
    att = _fox_sample(page_table, qbd, seq_major(kb), seq_major(vb), lf_new,
                      cache_k[lyr].reshape(n_pool, page, d_att), cache_v[lyr].reshape(n_pool, page, d_att),
                      jnp.transpose(cache_logf[lyr], (0, 2, 1)), n_att, hd, nt, ch)
    att_t = jnp.transpose(att[:, :nt, :], (1, 0, 2)).reshape(nt * n_s, d_att)
    h1 = _merge(xs2, y_ssm.reshape(nt * n_s, d_inner), att_t, gates, p, alpha, tm_merge)
    prev_f = jnp.transpose(state_conv_ffn[lyr], (1, 0, 2))
    y_s, tail_s = _ffn(h1.reshape(nt, n_s, d_model), prev_f, p, alpha, True, sbf)
    back = lambda a, *tail: jnp.transpose(a.reshape(nt, n_s, *tail), (1, 0) + tuple(range(2, 2 + len(tail))))
    y_s = back(y_s, d_model)
    k_s = back(kf, n_att, hd)[None]
    v_s = back(vf, n_att, hd)[None]
    lf_s = back(dtf[:, n_h:n_h + n_att], n_att)[None]
    cs_s = jnp.transpose(jnp.concatenate([prev_t, xbc_t], axis=0)[-(ssm_w - 1):], (1, 0, 2))[None]
    ss_s = ss_new.reshape(1, n_s, n_h, p_dim, d_state)
    cf_s = jnp.transpose(tail_s, (1, 0, 2))[None]
    y_p = y_p.reshape(n_p, L, d_model)
    return (y_p, y_s, k_p, v_p, lf_p, cs_p, ss_p, cf_p, k_s, v_s, lf_s, cs_s, ss_s, cf_s)


def kernel(x_prompt, x_sample, cache_k, cache_v, cache_logf, state_conv_ssm, state_ssm, state_conv_ffn, page_table, w_in, conv_ssm_w, conv_ssm_b, dt_bias, a_log, d_skip, ssm_norm_w, f_bias, gate_bias, w_ssm_o, w_att_o, w_o, ln1_g, ln1_b, w_ffn_up, ffn_conv_w, ffn_conv_b, w_ffn_down, ln2_g, ln2_b):
    return _forward(x_prompt, x_sample, cache_k, cache_v, cache_logf, state_conv_ssm, state_ssm, state_conv_ffn, page_table,
                    w_in, conv_ssm_w, conv_ssm_b, dt_bias, a_log, d_skip, ssm_norm_w, f_bias, gate_bias, w_ssm_o, w_att_o,
                    w_o, ln1_g, ln1_b, w_ffn_up, ffn_conv_w, ffn_conv_b, w_ffn_down, ln2_g, ln2_b)
```

```python
import functools
import math

import jax
import jax.numpy as jnp
from jax import lax
from jax.experimental import pallas as pl
from jax.experimental.pallas import tpu as pltpu

F32 = jnp.float32
BF16 = jnp.bfloat16

LN_EPS = 1e-5
RMS_EPS = 1e-5
SSD_CHUNK = 128
N_GROUPS = 2
NEG_BIG = -1e30
LOG2E = math.log2(math.e)
V7X_VMEM_BYTES = 64 * 1024 * 1024
VMEM_LIMIT = V7X_VMEM_BYTES - 8 * 1024 * 1024
LANES = 128
SUBLANES = 8
NEW_KEY_SLOTS = 16


def _dot(a, b):
    return jnp.dot(a, b, preferred_element_type=F32)


def _dot_nt(a, b):
    return lax.dot_general(a, b, (((1,), (1,)), ((), ())), preferred_element_type=F32)


def _split3(a):
    hi = a.astype(BF16)
    r1 = a - hi.astype(F32)
    mid = r1.astype(BF16)
    lo = (r1 - mid.astype(F32)).astype(BF16)
    return hi, mid, lo


def _dot3_l(a, m):
    hi, mid, lo = _split3(a)
    return _dot(hi, m) + _dot(mid, m) + _dot(lo, m)


def _dot3_r(m, a):
    hi, mid, lo = _split3(a)
    return _dot(m, hi) + _dot(m, mid) + _dot(m, lo)


def _silu(x):
    return x * (1.0 / (1.0 + jnp.exp(-x)))


def _sigmoid(x):
    return 1.0 / (1.0 + jnp.exp(-x))


def _layer_norm(x, g, b):
    mu = jnp.mean(x, axis=-1, keepdims=True)
    xc = x - mu
    var = jnp.mean(xc * xc, axis=-1, keepdims=True)
    return xc * lax.rsqrt(var + LN_EPS) * g + b


def _const_spec(shape):
    nd = len(shape)
    return pl.BlockSpec(shape, lambda *_: (0,) * nd, pipeline_mode=pl.Buffered(1))


def _params(sem):
    return pltpu.CompilerParams(dimension_semantics=sem, vmem_limit_bytes=VMEM_LIMIT)


def _in_proj_kernel(n_dt, n_att, x_ref, wz, wxbc, wq, wk, wv, wg, wsm, bsm,
                    z_o, xbc_o, q_o, qt_o, kb_o, kt_o, vb_o, vt_o, vtb_o, g_o, dtf_o, lft_o):
    xb = x_ref[...].astype(BF16)
    z_o[...] = _dot(xb, wz[...]).astype(BF16)
    xbc_o[...] = _dot(xb, wxbc[...])
    q = _dot(xb, wq[...])
    q_o[...] = q.astype(BF16)
    qt_o[0] = q.T.astype(BF16)
    k = _dot(xb, wk[...])
    kb_o[...] = k.astype(BF16)
    kt_o[0] = k.T
    v = _dot(xb, wv[...])
    vb_o[...] = v.astype(BF16)
    vt = v.T
    vt_o[0] = vt
    vtb_o[0] = vt.astype(BF16)
    g_o[...] = _dot(xb, wg[...]).astype(BF16)
    s = bsm[...] + _dot(xb, wsm[...])
    lane = lax.broadcasted_iota(jnp.int32, s.shape, 1)
    t = jnp.log1p(jnp.exp(-jnp.abs(s)))
    dtf = jnp.where(lane < n_dt, jnp.maximum(s, 0.0) + t, jnp.minimum(s, 0.0) - t)
    dtf_o[...] = dtf
    lft_o[0] = dtf.T[n_dt:n_dt + n_att, :]


def _in_proj(x3d, w, tm):
    nb, lb, d = x3d.shape
    m = nb * lb
    tm = min(tm, lb)
    tps = lb // tm
    n_att = w["n_att"]
    widths = [w[k].shape[1] for k in ("z", "xbc", "q", "k", "v", "g", "sm")]
    row = lambda i: (i, 0)
    tr = lambda i: (i // tps, 0, i % tps)
    in_specs = [pl.BlockSpec((tm, d), row)] + [_const_spec((d, wd)) for wd in widths] + [_const_spec((1, LANES))]
    da = w["k"].shape[1]
    outs = [("z", BF16, None), ("xbc", F32, None), ("q", BF16, None), ("q", BF16, da), ("k", BF16, None), ("k", F32, da),
            ("v", BF16, None), ("v", F32, da), ("v", BF16, da), ("g", BF16, None), ("sm", F32, None), ("sm", F32, n_att)]
    out_shape, out_specs = [], []
    for k, dt, trows in outs:
        if trows is None:
            out_shape.append(jax.ShapeDtypeStruct((m, w[k].shape[1]), dt))
            out_specs.append(pl.BlockSpec((tm, w[k].shape[1]), row))
        else:
            out_shape.append(jax.ShapeDtypeStruct((nb, trows, lb), dt))
            out_specs.append(pl.BlockSpec((1, trows, tm), tr))
    return pl.pallas_call(
        functools.partial(_in_proj_kernel, w["n_dt"], n_att),
        grid=(m // tm,), in_specs=in_specs, out_specs=out_specs, out_shape=out_shape,
        compiler_params=_params(("parallel",)), name="in_proj",
    )(x3d.reshape(m, d), w["z"], w["xbc"], w["q"], w["k"], w["v"], w["g"], w["sm"], w["bsm"])


def _ssd_prompt_kernel(n_h, p_dim, d_state, n_att,
                       xbc_ref, dtf_ref, z_ref, kb_ref, cw_ref, cb_ref, alog_ref, dskip_ref, nw_ref, e_ref, tri_ref, sel_ref,
                       y_o, kx_o, hout_o, ext, ht, carry, ysc):
    c = pl.program_id(1)
    nc = pl.num_programs(1)
    T = xbc_ref.shape[1]
    d_inner = n_h * p_dim
    gw = N_GROUPS * d_state
    hpg = n_h // N_GROUPS

    @pl.when(c == 0)
    def _():
        ext[0:SUBLANES, :] = jnp.zeros((SUBLANES, ext.shape[1]), F32)
        ht[...] = jnp.zeros(ht.shape, F32)
        carry[...] = jnp.zeros(carry.shape, F32)

    ext[SUBLANES:SUBLANES + T, :] = xbc_ref[0]
    width = cw_ref.shape[0]
    conv = cb_ref[...] + ext[SUBLANES:SUBLANES + T, :] * cw_ref[width - 1:width, :]
    for j in range(1, width):
        conv = conv + ext[SUBLANES - j:SUBLANES - j + T, :] * cw_ref[width - 1 - j:width - j, :]
    ext[0:SUBLANES, :] = ext[T:T + SUBLANES, :]
    act = _silu(conv)
    xs = act[:, :d_inner]
    bm = act[:, d_inner:d_inner + gw]
    cm = act[:, d_inner + gw:d_inner + 2 * gw]

    dtf = dtf_ref[0]
    lane = lax.broadcasted_iota(jnp.int32, dtf.shape, 1)
    a_row = -jnp.exp(alog_ref[...])
    dt = jnp.where(lane < n_h, dtf, 0.0)
    comb = jnp.where(lane < n_h, dtf * a_row, jnp.where(lane < n_h + n_att, dtf, 0.0))
    cs = _dot3_r(tri_ref[...], comb)
    cs_t = cs.T
    acs = jnp.where(lane < n_h, cs, 0.0)

    clf = jnp.where((lane >= n_h) & (lane < n_h + n_att), cs + carry[0:1, :], 0.0)
    carry[...] = jnp.broadcast_to(clf[T - 1:T, :], carry.shape)
    pieces = _split3(clf * (-LOG2E))
    ext_k = _dot(pieces[0], sel_ref[0]) + _dot(pieces[1], sel_ref[1]) + _dot(pieces[2], sel_ref[2])
    kb = kb_ref[0]
    for pr in range(kx_o.shape[1]):
        kx_o[0, pr] = jnp.concatenate([kb[:, pr * LANES:(pr + 1) * LANES],
                                       ext_k[:, pr * LANES:(pr + 1) * LANES].astype(BF16)], axis=1)

    e = e_ref[...]
    dt_e = _dot3_l(dt, e)
    acs_e = _dot3_l(acs, e)
    acs_last = acs_e[T - 1:T, :]
    xdt = xs * dt_e

    row_i = lax.broadcasted_iota(jnp.int32, (T, T), 0)
    col_i = lax.broadcasted_iota(jnp.int32, (T, T), 1)
    causal = col_i <= row_i
    half = lax.broadcasted_iota(jnp.int32, (T, 2 * p_dim), 1) < p_dim
    for g in range(N_GROUPS):
        cg = cm[:, g * d_state:(g + 1) * d_state].astype(BF16)
        bg = bm[:, g * d_state:(g + 1) * d_state].astype(BF16)
        cbm = _dot_nt(cg, bg)
        for pr in range(hpg // 2):
            h0 = g * hpg + 2 * pr
            ms = []
            for h in (h0, h0 + 1):
                seg = cs[:, h:h + 1] - cs_t[h:h + 1, :]
                dec = jnp.exp(jnp.where(causal, seg, NEG_BIG))
                ms.append((cbm * dec).astype(BF16))
            lo = h0 * p_dim
            xp = xdt[:, lo:lo + 2 * p_dim].astype(BF16)
            ysc[:, lo:lo + 2 * p_dim] = jnp.where(half, _dot(ms[0], xp), _dot(ms[1], xp))

    ht_old = ht[...]
    ht_b = ht_old.astype(BF16)
    xw = (xdt * jnp.exp(acs_last - acs_e)).astype(BF16)
    gl = hpg * p_dim
    y_off = []
    s_new = []
    for g in range(N_GROUPS):
        cg = cm[:, g * d_state:(g + 1) * d_state].astype(BF16)
        y_off.append(_dot(cg, ht_b[:, g * gl:(g + 1) * gl]))
        bgt = bm[:, g * d_state:(g + 1) * d_state].T.astype(BF16)
        s_new.append(_dot(bgt, xw[:, g * gl:(g + 1) * gl]))
    y_off = jnp.concatenate(y_off, axis=1) * jnp.exp(acs_e)
    ht_new = ht_old * jnp.exp(acs_last) + jnp.concatenate(s_new, axis=1)
    ht[...] = ht_new

    y = ysc[...] + y_off + xs * dskip_ref[...]
    y = y * _silu(z_ref[0].astype(F32))
    outs = []
    for g in range(N_GROUPS):
        yg = y[:, g * gl:(g + 1) * gl]
        outs.append(yg * lax.rsqrt(jnp.mean(yg * yg, axis=-1, keepdims=True) + RMS_EPS))
    y_o[0] = (jnp.concatenate(outs, axis=1) * nw_ref[...]).astype(y_o.dtype)

    @pl.when(c == nc - 1)
    def _():
        hout_o[0] = ht_new.T


def _ssd_prompt(xbc, dtf, z, kb, p, n_h, p_dim, d_state, n_att):
    n, L, cdim = xbc.shape
    d_att = kb.shape[2]
    n_pairs = d_att // LANES
    T = SSD_CHUNK if L % SSD_CHUNK == 0 else L
    d_inner = n_h * p_dim
    tri = jnp.tril(jnp.ones((T, T), F32)).astype(BF16)
    blk = lambda b, c: (b, c, 0)
    heads = jnp.arange(n_att)
    sel = jnp.zeros((3, LANES, d_att), F32)
    for piece in range(3):
        sel = sel.at[piece, n_h + heads, (heads // 2) * LANES + 3 * (heads % 2) + piece].set(1.0)
    sel = sel.astype(BF16)
    in_specs = [pl.BlockSpec((1, T, cdim), blk), pl.BlockSpec((1, T, LANES), blk), pl.BlockSpec((1, T, d_inner), blk),
                pl.BlockSpec((1, T, d_att), blk),
                _const_spec(p["conv_w"].shape), _const_spec((1, cdim)), _const_spec((1, LANES)),
                _const_spec((1, d_inner)), _const_spec((1, d_inner)), _const_spec((LANES, d_inner)), _const_spec((T, T)),
                _const_spec((3, LANES, d_att))]
    out_shape = [jax.ShapeDtypeStruct((n, L, d_inner), BF16), jax.ShapeDtypeStruct((n, n_pairs, L, 2 * LANES), BF16),
                 jax.ShapeDtypeStruct((n, d_inner, d_state), F32)]
    out_specs = [pl.BlockSpec((1, T, d_inner), blk), pl.BlockSpec((1, n_pairs, T, 2 * LANES), lambda b, c: (b, 0, c, 0)),
                 pl.BlockSpec((1, d_inner, d_state), lambda b, c: (b, 0, 0))]
    scratch = [pltpu.VMEM((T + SUBLANES, cdim), F32), pltpu.VMEM((d_state, d_inner), F32),
               pltpu.VMEM((SUBLANES, LANES), F32), pltpu.VMEM((T, d_inner), F32)]
    return pl.pallas_call(
        functools.partial(_ssd_prompt_kernel, n_h, p_dim, d_state, n_att),
        grid=(n, L // T), in_specs=in_specs, out_specs=out_specs, out_shape=out_shape, scratch_shapes=scratch,
        compiler_params=_params(("parallel", "arbitrary")), name="ssd_prompt",
    )(xbc, dtf, z, kb, p["conv_w"], p["conv_b"], p["a_log"], p["d_skip_e"], p["norm_w"], p["e_heads"], tri, sel)


def _fox_prompt_kernel(hd, qt_ref, kx_ref, vt_ref, o_ref, qx, m_s, acc_s):
    i = pl.program_id(2)
    tq = qt_ref.shape[1]
    tk = tq
    n_ones = acc_s.shape[1] - hd

    qt = qt_ref[...].astype(F32)
    row = lax.broadcasted_iota(jnp.int32, qt.shape, 0)
    for hh in range(2):
        qm = jnp.where((row >= hh * hd) & (row < (hh + 1) * hd), qt, 0.0)
        sel = jnp.where((row >= 3 * hh) & (row < 3 * hh + 3), 1.0, 0.0)
        qx[hh] = jnp.concatenate([qm, sel], axis=0).astype(BF16)
    m_s[...] = jnp.full(m_s.shape, NEG_BIG, F32)
    acc_s[...] = jnp.zeros(acc_s.shape, F32)
    ones = jnp.ones((n_ones, tk), BF16)

    def block(j, masked):
        off = pl.multiple_of(j * tk, tk)
        kxj = kx_ref[pl.ds(off, tk), :]
        vtj = vt_ref[:, pl.ds(off, tk)]
        for hh in range(2):
            s = _dot(kxj, qx[hh])
            if masked:
                key_i = lax.broadcasted_iota(jnp.int32, (tk, tq), 0)
                qry_i = lax.broadcasted_iota(jnp.int32, (tk, tq), 1)
                s = jnp.where(key_i <= qry_i, s, NEG_BIG)
            m_old = m_s[hh]
            m_new = jnp.maximum(m_old, jnp.max(s, axis=0, keepdims=True))
            alpha = jnp.exp2(m_old - m_new)
            pm = jnp.exp2(s - m_new).astype(BF16)
            vext = jnp.concatenate([vtj[hh * hd:(hh + 1) * hd, :], ones], axis=0)
            acc_s[hh] = alpha * acc_s[hh] + _dot(vext, pm)
            m_s[hh] = m_new

    def body(j, carry):
        block(j, False)
        return carry

    lax.fori_loop(0, i, body, 0)
    block(i, True)
    outs = [acc_s[hh][:hd, :] / acc_s[hh][hd:hd + 1, :] for hh in range(2)]
    o_ref[...] = jnp.concatenate(outs, axis=0).astype(o_ref.dtype)


def _fox_prompt(qt, kx, vt, hd, blk):
    n, da, L = qt.shape
    n_pairs = da // (2 * hd)
    t = min(blk, L)
    nb = L // t
    in_specs = [pl.BlockSpec((None, 2 * hd, t), lambda b, p, i: (b, p, i)),
                pl.BlockSpec((None, None, L, 4 * hd), lambda b, p, i: (b, p, 0, 0)),
                pl.BlockSpec((None, 2 * hd, L), lambda b, p, i: (b, p, 0))]
    out_specs = pl.BlockSpec((None, 2 * hd, t), lambda b, p, i: (b, p, i))
    scratch = [pltpu.VMEM((2, 4 * hd, t), BF16), pltpu.VMEM((2, 1, t), F32), pltpu.VMEM((2, hd + 2 * SUBLANES, t), F32)]
    return pl.pallas_call(
        functools.partial(_fox_prompt_kernel, hd),
        grid=(n, n_pairs, nb), in_specs=in_specs, out_specs=out_specs,
        out_shape=jax.ShapeDtypeStruct((n, da, L), BF16), scratch_shapes=scratch,
        compiler_params=_params(("parallel", "parallel", "arbitrary")), name="fox_prompt",
    )(qt, kx, vt)


def _merge_kernel(alpha, att_transposed, x_ref, ys_ref, att_ref, g_ref, wso, wao, wo, gb, lg, lb, h_o):
    d = x_ref.shape[1]
    a = _dot(ys_ref[...], wso[...])
    if att_transposed:
        att = att_ref[0].astype(F32).T.astype(BF16)
    else:
        att = att_ref[...]
    b = _dot(att, wao[...])
    gt = _sigmoid(g_ref[...].astype(F32) + gb[...])
    merged = gt[:, :d] * a + gt[:, d:] * b
    o = _dot(merged.astype(BF16), wo[...])
    h_o[...] = _layer_norm(alpha * x_ref[...] + o, lg[...], lb[...])


def _merge(x2d, ys, att, gates, p, alpha, tm):
    m, d = x2d.shape
    att_transposed = att.ndim == 3
    tm = min(tm, att.shape[2] if att_transposed else m)
    row = lambda i: (i, 0)
    ins = [x2d, ys, att, gates]
    consts = [p["w_ssm_o"], p["w_att_o"], p["w_o"], p["gate_bias"], p["ln1_g"], p["ln1_b"]]
    in_specs = [pl.BlockSpec((tm, a.shape[1]), row) for a in ins] + [_const_spec(c.shape) for c in consts]
    if att_transposed:
        tps = att.shape[2] // tm
        in_specs[2] = pl.BlockSpec((1, att.shape[1], tm), lambda i: (i // tps, 0, i % tps))
    return pl.pallas_call(
        functools.partial(_merge_kernel, alpha, att_transposed),
        grid=(m // tm,), in_specs=in_specs, out_specs=pl.BlockSpec((tm, d), row),
        out_shape=jax.ShapeDtypeStruct((m, d), F32),
        compiler_params=_params(("parallel",)), name="merge",
    )(*ins, *consts)


def _ffn_kernel(alpha, shift, n_col_chunks, h_ref, prev_ref, wup, cw, cb, wdn, lg, lb, o_ref, tail_o, ext, act):
    c = pl.program_id(1)
    tm = h_ref.shape[0] * h_ref.shape[1]
    dff = cw.shape[1]
    width = cw.shape[0]
    r = ext.shape[0] - tm
    cwid = dff // n_col_chunks

    @pl.when(c == 0)
    def _():
        ext[0:r, :] = prev_ref[...].reshape(r, dff)

    h = h_ref[...].reshape(tm, h_ref.shape[2])
    hb = h.astype(BF16)
    for ch in range(n_col_chunks):
        lo = ch * cwid
        ext[r:r + tm, lo:lo + cwid] = _dot(hb, wup[:, lo:lo + cwid])
        val = _dot(hb, wup[:, dff + lo:dff + lo + cwid])
        conv = cb[:, lo:lo + cwid] + ext[r:r + tm, lo:lo + cwid] * cw[width - 1:width, lo:lo + cwid]
        for j in range(1, width):
            conv = conv + ext[r - j * shift:r - j * shift + tm, lo:lo + cwid] * cw[width - 1 - j:width - j, lo:lo + cwid]
        gelu = 0.5 * conv * (1.0 + lax.erf(conv * math.sqrt(0.5)))
        act[:, lo:lo + cwid] = (gelu * val).astype(BF16)
    tail = ext[tm:tm + r, :]
    ext[0:r, :] = tail
    tail_o[...] = tail.reshape(tail_o.shape)
    o_ref[...] = _layer_norm(alpha * h + _dot(act[...], wdn[...]), lg[...], lb[...]).reshape(o_ref.shape)


def _ffn(h3, prev, p, alpha, time_major, tile):
    d = h3.shape[2]
    dff = p["ffn_conv_w"].shape[1]
    if time_major:
        nt, n_s, _ = h3.shape
        tile = min(tile, n_s)
        hblk, pblk = (nt, tile, d), (prev.shape[0], tile, dff)
        hmap = pmap = lambda b, c: (0, b, 0)
        grid, shift = (n_s // tile, 1), tile
    else:
        n, L, _ = h3.shape
        tile = min(tile, L)
        hblk, pblk = (1, tile, d), (1, prev.shape[1], dff)
        hmap, pmap = (lambda b, c: (b, c, 0)), (lambda b, c: (b, 0, 0))
        grid, shift = (n, L // tile), 1
    tm, r = hblk[0] * hblk[1], pblk[0] * pblk[1]
    consts = [p["w_ffn_up"], p["ffn_conv_w"], p["ffn_conv_b"], p["w_ffn_down"], p["ln2_g"], p["ln2_b"]]
    in_specs = [pl.BlockSpec(hblk, hmap), pl.BlockSpec(pblk, pmap)] + [_const_spec(cst.shape) for cst in consts]
    out_shape = [jax.ShapeDtypeStruct(h3.shape, F32), jax.ShapeDtypeStruct(prev.shape, F32)]
    out_specs = [pl.BlockSpec(hblk, hmap), pl.BlockSpec(pblk, pmap)]
    scratch = [pltpu.VMEM((tm + r, dff), F32), pltpu.VMEM((tm, dff), BF16)]
    return pl.pallas_call(
        functools.partial(_ffn_kernel, alpha, shift, 2),
        grid=grid, in_specs=in_specs, out_specs=out_specs, out_shape=out_shape, scratch_shapes=scratch,
        compiler_params=_params(("parallel", "arbitrary")), name="ffn",
    )(h3, prev, *consts)


def _ssd_sample_kernel(n_h, p_dim, d_state,
                       xbc_ref, prev_ref, dtf_ref, z_ref, st_ref, cw_ref, cb_ref, alog_ref, dskip_ref, nw_ref, e_ref,
                       y_o, st_o, xw_s, b_s, c_s, dec_s, yoff_s):
    nt, sb, cdim = xbc_ref.shape
    width = cw_ref.shape[0]
    d_inner = n_h * p_dim
    gw = N_GROUPS * d_state
    hpg = n_h // N_GROUPS
    gl = hpg * p_dim
    rows = nt * sb

    a_row = -jnp.exp(alog_ref[...])
    xin = [prev_ref[j] for j in range(width - 1)] + [xbc_ref[t] for t in range(nt)]
    xs, bm, cm, dts, acs = [], [], [], [], []
    run = None
    for t in range(nt):
        conv = cb_ref[...]
        for j in range(width):
            conv = conv + xin[t + j] * cw_ref[j:j + 1, :]
        act = _silu(conv)
        xs.append(act[:, :d_inner])
        bm.append(act[:, d_inner:d_inner + gw])
        cm.append(act[:, d_inner + gw:d_inner + 2 * gw])
        dtf = dtf_ref[t]
        lane = lax.broadcasted_iota(jnp.int32, dtf.shape, 1)
        dt = jnp.where(lane < n_h, dtf, 0.0)
        run = dt * a_row if run is None else run + dt * a_row
        dts.append(dt)
        acs.append(run)
    stacked = jnp.concatenate(dts + acs, axis=0)
    exp_all = _dot3_l(stacked, e_ref[...])
    dt_e = [exp_all[t * sb:(t + 1) * sb] for t in range(nt)]
    acs_e = [exp_all[(nt + t) * sb:(nt + t + 1) * sb] for t in range(nt)]
    xdt = [xs[t] * dt_e[t] for t in range(nt)]

    cbf = [cm[t].astype(BF16).astype(F32) for t in range(nt)]
    bbf = [bm[t].astype(BF16).astype(F32) for t in range(nt)]
    y_diag = []
    for t in range(nt):
        acc = jnp.zeros((sb, d_inner), F32)
        for s in range(t + 1):
            parts = []
            for g in range(N_GROUPS):
                dotg = jnp.sum(cbf[t][:, g * d_state:(g + 1) * d_state] * bbf[s][:, g * d_state:(g + 1) * d_state],
                               axis=-1, keepdims=True)
                w = jnp.exp(acs_e[t][:, g * gl:(g + 1) * gl] - acs_e[s][:, g * gl:(g + 1) * gl])
                parts.append(dotg * w * xdt[s][:, g * gl:(g + 1) * gl])
            acc = acc + jnp.concatenate(parts, axis=1)
        y_diag.append(acc)

    pad = xw_s.shape[0] - rows
    for t in range(nt):
        xw_s[t * sb:(t + 1) * sb, :] = xdt[t] * jnp.exp(acs_e[nt - 1] - acs_e[t])
        b_s[t * sb:(t + 1) * sb, :] = bm[t]
        c_s[t * sb:(t + 1) * sb, :] = cm[t]
    if pad:
        xw_s[rows:, :] = jnp.zeros((pad, d_inner), F32)
        b_s[rows:, :] = jnp.zeros((pad, gw), F32)
        c_s[rows:, :] = jnp.zeros((pad, gw), F32)
    dec_s[...] = jnp.exp(acs[nt - 1])
    yoff_s[...] = jnp.zeros(yoff_s.shape, F32)
    xw_t = xw_s[...].T.astype(BF16)
    rp = xw_s.shape[0]
    row_id = lax.broadcasted_iota(jnp.int32, (rp, 1), 0)

    def per_seq(j, carry):
        sel = (row_id % sb) == j
        h0 = st_ref[j]
        h0b = h0.astype(BF16)
        dec = dec_s[pl.ds(j, 1), :]
        bsel = jnp.where(sel, b_s[...], 0.0).astype(BF16)
        csel = jnp.where(sel, c_s[...], 0.0).astype(BF16)
        for g in range(N_GROUPS):
            yo = _dot_nt(csel[:, g * d_state:(g + 1) * d_state], h0b[g * gl:(g + 1) * gl, :])
            yoff_s[:, g * gl:(g + 1) * gl] = yoff_s[:, g * gl:(g + 1) * gl] + yo
            upd = _dot(xw_t[g * gl:(g + 1) * gl, :], bsel[:, g * d_state:(g + 1) * d_state])
            for r in range(hpg):
                h = g * hpg + r
                lo = h * p_dim
                dcol = jnp.broadcast_to(dec[:, h:h + 1], (p_dim, d_state))
                st_o[j, lo:lo + p_dim, :] = h0[lo:lo + p_dim, :] * dcol + upd[r * p_dim:(r + 1) * p_dim, :]
        return carry

    lax.fori_loop(0, sb, per_seq, 0)

    for t in range(nt):
        y = y_diag[t] + yoff_s[t * sb:(t + 1) * sb, :] * jnp.exp(acs_e[t]) + xs[t] * dskip_ref[...]
        y = y * _silu(z_ref[t].astype(F32))
        outs = []
        for g in range(N_GROUPS):
            yg = y[:, g * gl:(g + 1) * gl]
            outs.append(yg * lax.rsqrt(jnp.mean(yg * yg, axis=-1, keepdims=True) + RMS_EPS))
        y_o[t] = (jnp.concatenate(outs, axis=1) * nw_ref[...]).astype(y_o.dtype)


def _ssd_sample(xbc_t, prev_t, dtf_t, z_t, state, p, n_h, p_dim, d_state, sb):
    nt, s, cdim = xbc_t.shape
    sb = min(sb, s)
    d_inner = n_h * p_dim
    gw = N_GROUPS * d_state
    rows_pad = max(LANES, -(-nt * sb // LANES) * LANES)
    tb = lambda i: (0, i, 0)
    in_specs = [pl.BlockSpec((nt, sb, cdim), tb), pl.BlockSpec((prev_t.shape[0], sb, cdim), tb),
                pl.BlockSpec((nt, sb, LANES), tb), pl.BlockSpec((nt, sb, d_inner), tb),
                pl.BlockSpec((sb, d_inner, d_state), lambda i: (i, 0, 0)),
                _const_spec(p["conv_w"].shape), _const_spec((1, cdim)), _const_spec((1, LANES)),
                _const_spec((1, d_inner)), _const_spec((1, d_inner)), _const_spec((LANES, d_inner))]
    out_shape = [jax.ShapeDtypeStruct((nt, s, d_inner), BF16), jax.ShapeDtypeStruct(state.shape, F32)]
    out_specs = [pl.BlockSpec((nt, sb, d_inner), tb), pl.BlockSpec((sb, d_inner, d_state), lambda i: (i, 0, 0))]
    scratch = [pltpu.VMEM((rows_pad, d_inner), F32), pltpu.VMEM((rows_pad, gw), F32), pltpu.VMEM((rows_pad, gw), F32),
               pltpu.VMEM((sb, LANES), F32), pltpu.VMEM((rows_pad, d_inner), F32)]
    return pl.pallas_call(
        functools.partial(_ssd_sample_kernel, n_h, p_dim, d_state),
        grid=(s // sb,), in_specs=in_specs, out_specs=out_specs, out_shape=out_shape, scratch_shapes=scratch,
        compiler_params=_params(("parallel",)), name="ssd_sample",
    )(xbc_t, prev_t, dtf_t, z_t, state, p["conv_w"], p["conv_b"], p["a_log"], p["d_skip_e"], p["norm_w"], p["e_heads"])


def _fox_sample_kernel(n_pages, ch, page, n_heads, hd, nt,
                       pt_ref, qbd_ref, kn_ref, vn_ref, lfn_ref, su_ref, k_hbm, v_hbm, lf_hbm,
                       o_ref, kbuf, vbuf, lfbuf, ksem, vsem, lfsem):
    s = pl.program_id(0)
    ns = pl.num_programs(0)
    n_chunks = n_pages // ch
    rows = nt * n_heads
    da = n_heads * hd

    def kv_copies(seq, chunk, slot):
        cps = []
        for i in range(ch):
            pid = pt_ref[seq, chunk * ch + i]
            dst = pl.ds(i * page, page)
            cps.append(pltpu.make_async_copy(k_hbm.at[pid], kbuf.at[slot, :, dst], ksem.at[slot]))
            cps.append(pltpu.make_async_copy(v_hbm.at[pid], vbuf.at[slot, :, dst], vsem.at[slot]))
        return cps

    def lf_copies(seq, slot):
        return [pltpu.make_async_copy(lf_hbm.at[pt_ref[seq, i]], lfbuf.at[slot, i], lfsem.at[slot]) for i in range(n_pages)]

    @pl.when(s == 0)
    def _():
        for cp in lf_copies(0, 0):
            cp.start()
        for cp in kv_copies(0, 0, 0):
            cp.start()

    ls = s % 2
    for cp in lf_copies(s, ls):
        cp.wait()

    @pl.when(s + 1 < ns)
    def _():
        for cp in lf_copies(s + 1, 1 - ls):
            cp.start()

    lf = lfbuf[ls]
    lf2 = lf.reshape(n_pages * n_heads, page)
    within = _dot3_l(lf2, su_ref[...]).reshape(n_pages, n_heads, page)
    tot = jnp.sum(lf, axis=-1, keepdims=True)
    bias_pages = [None] * n_pages
    run = jnp.zeros((n_heads, 1), F32)
    for i in reversed(range(n_pages)):
        bias_pages[i] = within[i] + run
        run = run + tot[i]

    qbd = qbd_ref[0]
    m_run = jnp.full((rows, 1), NEG_BIG, F32)
    l_run = jnp.zeros((rows, 1), F32)
    acc = jnp.zeros((rows, da), F32)
    for c in range(n_chunks):
        g = s * n_chunks + c
        slot = g % 2
        if c + 1 < n_chunks:
            for cp in kv_copies(s, c + 1, 1 - slot):
                cp.start()
        else:
            @pl.when(s + 1 < ns)
            def _():
                for cp in kv_copies(s + 1, 0, 1 - slot):
                    cp.start()
        for cp in kv_copies(s, c, slot):
            cp.wait()
        kc = kbuf[slot].astype(BF16)
        vc = vbuf[slot].astype(BF16)
        bias = jnp.concatenate(bias_pages[c * ch:(c + 1) * ch], axis=1)
        u = jnp.concatenate([bias] * nt, axis=0) + _dot(qbd, kc)
        m_new = jnp.maximum(m_run, jnp.max(u, axis=-1, keepdims=True))
        alpha = jnp.exp(m_run - m_new)
        pm = jnp.exp(u - m_new)
        l_run = alpha * l_run + jnp.sum(pm, axis=-1, keepdims=True)
        acc = alpha * acc + _dot_nt(pm.astype(BF16), vc)
        m_run = m_new

    lfn = lfn_ref[0]
    lane = lax.broadcasted_iota(jnp.int32, lfn.shape, 1)
    cn = jnp.zeros(lfn.shape, F32)
    for t in range(nt):
        cn = cn + jnp.where(lane >= t, lfn[:, t:t + 1], 0.0)
    n_slots = kn_ref.shape[1]
    nbias = jnp.concatenate([-cn[:, :n_slots]] * nt, axis=0)
    row_t = lax.broadcasted_iota(jnp.int32, (rows, n_slots), 0) // n_heads
    lane_r = lax.broadcasted_iota(jnp.int32, (rows, n_slots), 1)
    u = jnp.where(lane_r <= row_t, nbias + _dot_nt(qbd, kn_ref[0]), NEG_BIG)
    m_new = jnp.maximum(m_run, jnp.max(u, axis=-1, keepdims=True))
    alpha = jnp.exp(m_run - m_new)
    pm = jnp.exp(u - m_new)
    l_run = alpha * l_run + jnp.sum(pm, axis=-1, keepdims=True)
    acc = alpha * acc + _dot(pm.astype(BF16), vn_ref[0])
    out = acc / l_run
    col_h = lax.broadcasted_iota(jnp.int32, (rows, da), 1) // hd
    row_h = lax.broadcasted_iota(jnp.int32, (rows, da), 0) % n_heads
    out = jnp.where(col_h == row_h, out, 0.0)
    o_ref[0] = jnp.concatenate(
        [jnp.sum(out[t * n_heads:(t + 1) * n_heads], axis=0, keepdims=True) for t in range(nt)]
        + [jnp.zeros((SUBLANES - nt, da), F32)], axis=0).astype(o_ref.dtype)


def _fox_sample(page_table, qbd, k_new, v_new, lf_new, cache_k, cache_v, cache_lft, n_heads, hd, nt, ch):
    s, n_pages = page_table.shape
    page = cache_k.shape[2]
    da = n_heads * hd
    ch = min(ch, n_pages)
    while n_pages % ch:
        ch -= 1
    rows = nt * n_heads
    su = jnp.triu(jnp.ones((page, page), F32), k=1).T.astype(BF16)
    per_seq = lambda i, pt: (i, 0, 0)
    grid_spec = pltpu.PrefetchScalarGridSpec(
        num_scalar_prefetch=1, grid=(s,),
        in_specs=[pl.BlockSpec((1, rows, da), per_seq), pl.BlockSpec((1, k_new.shape[1], da), per_seq),
                  pl.BlockSpec((1, k_new.shape[1], da), per_seq), pl.BlockSpec((1, n_heads, LANES), per_seq),
                  pl.BlockSpec((page, page), lambda i, pt: (0, 0)),
                  pl.BlockSpec(memory_space=pl.ANY), pl.BlockSpec(memory_space=pl.ANY), pl.BlockSpec(memory_space=pl.ANY)],
        out_specs=pl.BlockSpec((1, SUBLANES, da), per_seq),
        scratch_shapes=[pltpu.VMEM((2, da, ch * page), F32), pltpu.VMEM((2, da, ch * page), F32),
                        pltpu.VMEM((2, n_pages, n_heads, page), F32),
                        pltpu.SemaphoreType.DMA((2,)), pltpu.SemaphoreType.DMA((2,)), pltpu.SemaphoreType.DMA((2,))])
    return pl.pallas_call(
        functools.partial(_fox_sample_kernel, n_pages, ch, page, n_heads, hd, nt),
        grid_spec=grid_spec, out_shape=jax.ShapeDtypeStruct((s, SUBLANES, da), BF16),
        compiler_params=_params(("arbitrary",)), name="fox_sample",
    )(page_table, qbd, k_new, v_new, lf_new, su, cache_k, cache_v, cache_lft)


def _prep_params(dims, w_in, conv_ssm_w, conv_ssm_b, dt_bias, a_log, d_skip, ssm_norm_w, f_bias, gate_bias,
                 w_ssm_o, w_att_o, w_o, ln1_g, ln1_b, w_ffn_up, ffn_conv_w, ffn_conv_b, w_ffn_down, ln2_g, ln2_b):
    d_inner, cdim, n_h, d_att, n_att, d_model, p_dim, hd = dims
    z_end = d_inner
    xbc_end = z_end + cdim
    dt_end = xbc_end + n_h
    q_end = dt_end + d_att
    k_end = q_end + d_att
    v_end = k_end + d_att
    f_end = v_end + n_att
    row = lambda a: a.reshape(1, -1).astype(F32)
    pad_lanes = lambda a: jnp.pad(a, ((0, 0), (0, LANES - a.shape[1])))
    scale = hd ** -0.5
    w = {
        "n_dt": n_h, "n_att": n_att,
        "z": w_in[:, :z_end].astype(BF16),
        "xbc": w_in[:, z_end:xbc_end].astype(BF16),
        "q": (w_in[:, dt_end:q_end] * scale).astype(BF16),
        "q_log2": (w_in[:, dt_end:q_end] * (scale * LOG2E)).astype(BF16),
        "k": w_in[:, q_end:k_end].astype(BF16),
        "v": w_in[:, k_end:v_end].astype(BF16),
        "g": w_in[:, f_end:].astype(BF16),
        "sm": pad_lanes(jnp.concatenate([w_in[:, xbc_end:dt_end], w_in[:, v_end:f_end]], axis=1)).astype(BF16),
        "bsm": pad_lanes(jnp.concatenate([row(dt_bias), row(f_bias)], axis=1)),
    }
    head_of_lane = jnp.arange(d_inner) // p_dim
    p = {
        "conv_w": conv_ssm_w.astype(F32), "conv_b": row(conv_ssm_b),
        "a_log": pad_lanes(row(a_log)),
        "d_skip_e": row(jnp.repeat(d_skip, p_dim)), "norm_w": row(ssm_norm_w),
        "e_heads": (jnp.arange(LANES)[:, None] == head_of_lane[None, :]).astype(BF16),
        "w_ssm_o": w_ssm_o.astype(BF16), "w_att_o": w_att_o.astype(BF16), "w_o": w_o.astype(BF16),
        "gate_bias": row(gate_bias), "ln1_g": row(ln1_g), "ln1_b": row(ln1_b),
        "w_ffn_up": w_ffn_up.astype(BF16), "ffn_conv_w": ffn_conv_w.astype(F32), "ffn_conv_b": row(ffn_conv_b),
        "w_ffn_down": w_ffn_down.astype(BF16), "ln2_g": row(ln2_g), "ln2_b": row(ln2_b),
    }
    return w, p


def _forward(x_prompt, x_sample, cache_k, cache_v, cache_logf, state_conv_ssm, state_ssm, state_conv_ffn, page_table,
             w_in, conv_ssm_w, conv_ssm_b, dt_bias, a_log, d_skip, ssm_norm_w, f_bias, gate_bias, w_ssm_o, w_att_o,
             w_o, ln1_g, ln1_b, w_ffn_up, ffn_conv_w, ffn_conv_b, w_ffn_down, ln2_g, ln2_b,
             tm_proj=256, tm_merge=512, tm_ffn=256, att_blk=512, sb=8, sbf=32, ch=16):
    depth = w_in.shape[0]
    assert depth == 1, "single-layer step"
    n_p, L, d_model = x_prompt.shape
    n_s, nt, _ = x_sample.shape
    n_pool, page, n_att, hd = cache_k.shape[1:]
    n_h, p_dim, d_state = state_ssm.shape[2:]
    d_inner = n_h * p_dim
    cdim = conv_ssm_w.shape[2]
    d_att = n_att * hd
    dff = ffn_conv_w.shape[2]
    ssm_w = conv_ssm_w.shape[1]
    ffn_w = ffn_conv_w.shape[1]
    alpha = (2.0 * depth) ** 0.25
    dims = (d_inner, cdim, n_h, d_att, n_att, d_model, p_dim, hd)
    lyr = 0
    w, p = _prep_params(dims, w_in[lyr], conv_ssm_w[lyr], conv_ssm_b[lyr], dt_bias[lyr], a_log[lyr], d_skip[lyr],
                        ssm_norm_w[lyr], f_bias[lyr], gate_bias[lyr], w_ssm_o[lyr], w_att_o[lyr], w_o[lyr], ln1_g[lyr],
                        ln1_b[lyr], w_ffn_up[lyr], ffn_conv_w[lyr], ffn_conv_b[lyr], w_ffn_down[lyr], ln2_g[lyr], ln2_b[lyr])

    xp2 = x_prompt.reshape(n_p * L, d_model)
    wp = dict(w, q=w["q_log2"])
    z, xbc, _, qt, kb, kt, _, vt, vtb, gates, dtf, lft = _in_proj(x_prompt, wp, tm_proj)
    xbc3 = xbc.reshape(n_p, L, cdim)
    y_ssm, kx, h_fin = _ssd_prompt(xbc3, dtf.reshape(n_p, L, LANES), z.reshape(n_p, L, d_inner), kb.reshape(n_p, L, d_att),
                                   p, n_h, p_dim, d_state, n_att)
    att_t = _fox_prompt(qt, kx, vtb, hd, att_blk)
    h1 = _merge(xp2, y_ssm.reshape(n_p * L, d_inner), att_t, gates, p, alpha, tm_merge)
    prev0 = jnp.zeros((n_p, SUBLANES, dff), F32)
    y_p, tail_p = _ffn(h1.reshape(n_p, L, d_model), prev0, p, alpha, False, tm_ffn)
    k_p = jnp.transpose(kt.reshape(1, n_p, n_att, hd, L), (0, 1, 4, 2, 3))
    v_p = jnp.transpose(vt.reshape(1, n_p, n_att, hd, L), (0, 1, 4, 2, 3))
    lf_p = jnp.transpose(lft.reshape(1, n_p, n_att, L), (0, 1, 3, 2))
    cs_p = xbc3[:, L - (ssm_w - 1):, :][None]
    ss_p = h_fin.reshape(1, n_p, n_h, p_dim, d_state)
    cf_p = tail_p[:, SUBLANES - (ffn_w - 1):, :][None]

    xs3 = jnp.transpose(x_sample, (1, 0, 2))
    xs2 = xs3.reshape(nt * n_s, d_model)
    z, xbc, q, _, kb, kt, vb, vt, _, gates, dtf, lft = _in_proj(xs3, w, tm_proj)
    xbc_t = xbc.reshape(nt, n_s, cdim)
    prev_t = jnp.transpose(state_conv_ssm[lyr], (1, 0, 2))
    y_ssm, ss_new = _ssd_sample(xbc_t, prev_t, dtf.reshape(nt, n_s, LANES), z.reshape(nt, n_s, d_inner),
                                state_ssm[lyr].reshape(n_s, d_inner, d_state), p, n_h, p_dim, d_state, sb)
    q_t = q.reshape(nt, n_s, n_att, hd)
    eye = jnp.eye(n_att, dtype=BF16)
    qbd = jnp.einsum("tshe,hg->sthge", q_t, eye).reshape(n_s, nt * n_att, d_att)
    seq_major = lambda a: jnp.pad(jnp.transpose(a.reshape(nt, n_s, d_att), (1, 0, 2)), ((0, 0), (0, NEW_KEY_SLOTS - nt), (0, 0)))
    lf_new = jnp.transpose(dtf[:, n_h:n_h + n_att].reshape(nt, n_s, n_att), (1, 2, 0))
    lf_new = jnp.pad(lf_new, ((0, 0), (0, 0), (0, LANES - nt)))
    att = _fox_sample(page_table, qbd, seq_major(kb), seq_major(vb), lf_new,
                      jnp.transpose(cache_k[lyr], (0, 2, 3, 1)).reshape(n_pool, d_att, page),
                      jnp.transpose(cache_v[lyr], (0, 2, 3, 1)).reshape(n_pool, d_att, page),
                      jnp.transpose(cache_logf[lyr], (0, 2, 1)), n_att, hd, nt, ch)
    att_t = jnp.transpose(att[:, :nt, :], (1, 0, 2)).reshape(nt * n_s, d_att)
    h1 = _merge(xs2, y_ssm.reshape(nt * n_s, d_inner), att_t, gates, p, alpha, tm_merge)
    prev_f = jnp.transpose(state_conv_ffn[lyr], (1, 0, 2))
    y_s, tail_s = _ffn(h1.reshape(nt, n_s, d_model), prev_f, p, alpha, True, sbf)
    back = lambda a, *tail: jnp.transpose(a.reshape(nt, n_s, *tail), (1, 0) + tuple(range(2, 2 + len(tail))))
    y_s = back(y_s, d_model)
    k_s = jnp.transpose(kt.reshape(1, nt, n_att, hd, n_s), (0, 4, 1, 2, 3))
    v_s = jnp.transpose(vt.reshape(1, nt, n_att, hd, n_s), (0, 4, 1, 2, 3))
    lf_s = jnp.transpose(lft.reshape(1, nt, n_att, n_s), (0, 3, 1, 2))
    cs_s = jnp.transpose(jnp.concatenate([prev_t, xbc_t], axis=0)[-(ssm_w - 1):], (1, 0, 2))[None]
    ss_s = ss_new.reshape(1, n_s, n_h, p_dim, d_state)
    cf_s = jnp.transpose(tail_s, (1, 0, 2))[None]
    y_p = y_p.reshape(n_p, L, d_model)
    return (y_p, y_s, k_p, v_p, lf_p, cs_p, ss_p, cf_p, k_s, v_s, lf_s, cs_s, ss_s, cf_s)


def kernel(x_prompt, x_sample, cache_k, cache_v, cache_logf, state_conv_ssm, state_ssm, state_conv_ffn, page_table, w_in, conv_ssm_w, conv_ssm_b, dt_bias, a_log, d_skip, ssm_norm_w, f_bias, gate_bias, w_ssm_o, w_att_o, w_o, ln1_g, ln1_b, w_ffn_up, ffn_conv_w, ffn_conv_b, w_ffn_down, ln2_g, ln2_b):
    return _forward(x_prompt, x_sample, cache_k, cache_v, cache_logf, state_conv_ssm, state_ssm, state_conv_ffn, page_table,
                    w_in, conv_ssm_w, conv_ssm_b, dt_bias, a_log, d_skip, ssm_norm_w, f_bias, gate_bias, w_ssm_o, w_att_o,
                    w_o, ln1_g, ln1_b, w_ffn_up, ffn_conv_w, ffn_conv_b, w_ffn_down, ln2_g, ln2_b)
```

```python
import functools
import math

import jax
import jax.numpy as jnp
from jax import lax
from jax.experimental import pallas as pl
from jax.experimental.pallas import tpu as pltpu

F32 = jnp.float32
BF16 = jnp.bfloat16

LN_EPS = 1e-5
RMS_EPS = 1e-5
SSD_CHUNK = 128
N_GROUPS = 2
NEG_BIG = -1e30
LOG2E = math.log2(math.e)
V7X_VMEM_BYTES = 64 * 1024 * 1024
VMEM_LIMIT = V7X_VMEM_BYTES - 8 * 1024 * 1024
LANES = 128
SUBLANES = 8
NEW_KEY_SLOTS = 16
FOX_COL_GROUP = 512


def _dot(a, b):
    return jnp.dot(a, b, preferred_element_type=F32)


def _dot_nt(a, b):
    return lax.dot_general(a, b, (((1,), (1,)), ((), ())), preferred_element_type=F32)


def _split3(a):
    hi = a.astype(BF16)
    r1 = a - hi.astype(F32)
    mid = r1.astype(BF16)
    lo = (r1 - mid.astype(F32)).astype(BF16)
    return hi, mid, lo


def _dot3_l(a, m):
    hi, mid, lo = _split3(a)
    return _dot(hi, m) + _dot(mid, m) + _dot(lo, m)


def _dot3_r(m, a):
    hi, mid, lo = _split3(a)
    return _dot(m, hi) + _dot(m, mid) + _dot(m, lo)


def _silu(x):
    return x * (1.0 / (1.0 + jnp.exp(-x)))


def _sigmoid(x):
    return 1.0 / (1.0 + jnp.exp(-x))


def _layer_norm(x, g, b):
    mu = jnp.mean(x, axis=-1, keepdims=True)
    xc = x - mu
    var = jnp.mean(xc * xc, axis=-1, keepdims=True)
    return xc * lax.rsqrt(var + LN_EPS) * g + b


def _const_spec(shape):
    nd = len(shape)
    return pl.BlockSpec(shape, lambda *_: (0,) * nd, pipeline_mode=pl.Buffered(1))


def _params(sem):
    return pltpu.CompilerParams(dimension_semantics=sem, vmem_limit_bytes=VMEM_LIMIT)


def _in_proj_kernel(n_dt, n_att, x_ref, wz, wxbc, wq, wk, wv, wg, wsm, bsm,
                    z_o, xbc_o, q_o, qt_o, kb_o, kt_o, vb_o, vt_o, vtb_o, g_o, dtf_o, lft_o):
    xb = x_ref[...].astype(BF16)
    z_o[...] = _dot(xb, wz[...]).astype(BF16)
    xbc_o[...] = _dot(xb, wxbc[...])
    q = _dot(xb, wq[...])
    q_o[...] = q.astype(BF16)
    qt_o[0] = q.T.astype(BF16)
    k = _dot(xb, wk[...])
    kb_o[...] = k.astype(BF16)
    kt_o[0] = k.T
    v = _dot(xb, wv[...])
    vb_o[...] = v.astype(BF16)
    vt = v.T
    vt_o[0] = vt
    vtb_o[0] = vt.astype(BF16)
    g_o[...] = _dot(xb, wg[...]).astype(BF16)
    s = bsm[...] + _dot(xb, wsm[...])
    lane = lax.broadcasted_iota(jnp.int32, s.shape, 1)
    t = jnp.log1p(jnp.exp(-jnp.abs(s)))
    dtf = jnp.where(lane < n_dt, jnp.maximum(s, 0.0) + t, jnp.minimum(s, 0.0) - t)
    dtf_o[...] = dtf
    lft_o[0] = dtf.T[n_dt:n_dt + n_att, :]


def _in_proj(x3d, w, tm):
    nb, lb, d = x3d.shape
    m = nb * lb
    tm = min(tm, lb)
    tps = lb // tm
    n_att = w["n_att"]
    widths = [w[k].shape[1] for k in ("z", "xbc", "q", "k", "v", "g", "sm")]
    row = lambda i: (i, 0)
    tr = lambda i: (i // tps, 0, i % tps)
    in_specs = [pl.BlockSpec((tm, d), row)] + [_const_spec((d, wd)) for wd in widths] + [_const_spec((1, LANES))]
    da = w["k"].shape[1]
    outs = [("z", BF16, None), ("xbc", F32, None), ("q", BF16, None), ("q", BF16, da), ("k", BF16, None), ("k", F32, da),
            ("v", BF16, None), ("v", F32, da), ("v", BF16, da), ("g", BF16, None), ("sm", F32, None), ("sm", F32, n_att)]
    out_shape, out_specs = [], []
    for k, dt, trows in outs:
        if trows is None:
            out_shape.append(jax.ShapeDtypeStruct((m, w[k].shape[1]), dt))
            out_specs.append(pl.BlockSpec((tm, w[k].shape[1]), row))
        else:
            out_shape.append(jax.ShapeDtypeStruct((nb, trows, lb), dt))
            out_specs.append(pl.BlockSpec((1, trows, tm), tr))
    return pl.pallas_call(
        functools.partial(_in_proj_kernel, w["n_dt"], n_att),
        grid=(m // tm,), in_specs=in_specs, out_specs=out_specs, out_shape=out_shape,
        compiler_params=_params(("parallel",)), name="in_proj",
    )(x3d.reshape(m, d), w["z"], w["xbc"], w["q"], w["k"], w["v"], w["g"], w["sm"], w["bsm"])


def _ssd_prompt_kernel(n_h, p_dim, d_state, n_att,
                       xbc_ref, dtf_ref, z_ref, kb_ref, cw_ref, cb_ref, alog_ref, dskip_ref, nw_ref, e_ref, tri_ref, sel_ref,
                       y_o, kx_o, hout_o, ext, ht, carry, ysc):
    c = pl.program_id(1)
    nc = pl.num_programs(1)
    T = xbc_ref.shape[1]
    d_inner = n_h * p_dim
    gw = N_GROUPS * d_state
    hpg = n_h // N_GROUPS

    @pl.when(c == 0)
    def _():
        ext[0:SUBLANES, :] = jnp.zeros((SUBLANES, ext.shape[1]), F32)
        ht[...] = jnp.zeros(ht.shape, F32)
        carry[...] = jnp.zeros(carry.shape, F32)

    ext[SUBLANES:SUBLANES + T, :] = xbc_ref[0]
    width = cw_ref.shape[0]
    conv = cb_ref[...] + ext[SUBLANES:SUBLANES + T, :] * cw_ref[width - 1:width, :]
    for j in range(1, width):
        conv = conv + ext[SUBLANES - j:SUBLANES - j + T, :] * cw_ref[width - 1 - j:width - j, :]
    ext[0:SUBLANES, :] = ext[T:T + SUBLANES, :]
    act = _silu(conv)
    xs = act[:, :d_inner]
    bm = act[:, d_inner:d_inner + gw]
    cm = act[:, d_inner + gw:d_inner + 2 * gw]

    dtf = dtf_ref[0]
    lane = lax.broadcasted_iota(jnp.int32, dtf.shape, 1)
    a_row = -jnp.exp(alog_ref[...])
    dt = jnp.where(lane < n_h, dtf, 0.0)
    comb = jnp.where(lane < n_h, dtf * a_row, jnp.where(lane < n_h + n_att, dtf, 0.0))
    cs = _dot3_r(tri_ref[...], comb)
    cs_t = cs.T
    acs = jnp.where(lane < n_h, cs, 0.0)

    clf = jnp.where((lane >= n_h) & (lane < n_h + n_att), cs + carry[0:1, :], 0.0)
    carry[...] = jnp.broadcast_to(clf[T - 1:T, :], carry.shape)
    pieces = _split3(clf * (-LOG2E))
    ext_k = _dot(pieces[0], sel_ref[0]) + _dot(pieces[1], sel_ref[1]) + _dot(pieces[2], sel_ref[2])
    kb = kb_ref[0]
    for pr in range(kx_o.shape[1]):
        kx_o[0, pr] = jnp.concatenate([kb[:, pr * LANES:(pr + 1) * LANES],
                                       ext_k[:, pr * LANES:(pr + 1) * LANES].astype(BF16)], axis=1)

    e = e_ref[...]
    dt_e = _dot3_l(dt, e)
    acs_e = _dot3_l(acs, e)
    acs_last = acs_e[T - 1:T, :]
    xdt = xs * dt_e

    row_i = lax.broadcasted_iota(jnp.int32, (T, T), 0)
    col_i = lax.broadcasted_iota(jnp.int32, (T, T), 1)
    causal = col_i <= row_i
    half = lax.broadcasted_iota(jnp.int32, (T, 2 * p_dim), 1) < p_dim
    for g in range(N_GROUPS):
        cg = cm[:, g * d_state:(g + 1) * d_state].astype(BF16)
        bg = bm[:, g * d_state:(g + 1) * d_state].astype(BF16)
        cbm = _dot_nt(cg, bg)
        for pr in range(hpg // 2):
            h0 = g * hpg + 2 * pr
            ms = []
            for h in (h0, h0 + 1):
                seg = cs[:, h:h + 1] - cs_t[h:h + 1, :]
                dec = jnp.exp(jnp.where(causal, seg, NEG_BIG))
                ms.append((cbm * dec).astype(BF16))
            lo = h0 * p_dim
            xp = xdt[:, lo:lo + 2 * p_dim].astype(BF16)
            ysc[:, lo:lo + 2 * p_dim] = jnp.where(half, _dot(ms[0], xp), _dot(ms[1], xp))

    ht_old = ht[...]
    ht_b = ht_old.astype(BF16)
    xw = (xdt * jnp.exp(acs_last - acs_e)).astype(BF16)
    gl = hpg * p_dim
    y_off = []
    s_new = []
    for g in range(N_GROUPS):
        cg = cm[:, g * d_state:(g + 1) * d_state].astype(BF16)
        y_off.append(_dot(cg, ht_b[:, g * gl:(g + 1) * gl]))
        bgt = bm[:, g * d_state:(g + 1) * d_state].T.astype(BF16)
        s_new.append(_dot(bgt, xw[:, g * gl:(g + 1) * gl]))
    y_off = jnp.concatenate(y_off, axis=1) * jnp.exp(acs_e)
    ht_new = ht_old * jnp.exp(acs_last) + jnp.concatenate(s_new, axis=1)
    ht[...] = ht_new

    y = ysc[...] + y_off + xs * dskip_ref[...]
    y = y * _silu(z_ref[0].astype(F32))
    outs = []
    for g in range(N_GROUPS):
        yg = y[:, g * gl:(g + 1) * gl]
        outs.append(yg * lax.rsqrt(jnp.mean(yg * yg, axis=-1, keepdims=True) + RMS_EPS))
    y_o[0] = (jnp.concatenate(outs, axis=1) * nw_ref[...]).astype(y_o.dtype)

    @pl.when(c == nc - 1)
    def _():
        hout_o[0] = ht_new.T


def _ssd_prompt(xbc, dtf, z, kb, p, n_h, p_dim, d_state, n_att):
    n, L, cdim = xbc.shape
    d_att = kb.shape[2]
    n_pairs = d_att // LANES
    T = SSD_CHUNK if L % SSD_CHUNK == 0 else L
    d_inner = n_h * p_dim
    tri = jnp.tril(jnp.ones((T, T), F32)).astype(BF16)
    blk = lambda b, c: (b, c, 0)
    heads = jnp.arange(n_att)
    sel = jnp.zeros((3, LANES, d_att), F32)
    for piece in range(3):
        sel = sel.at[piece, n_h + heads, (heads // 2) * LANES + 3 * (heads % 2) + piece].set(1.0)
    sel = sel.astype(BF16)
    in_specs = [pl.BlockSpec((1, T, cdim), blk), pl.BlockSpec((1, T, LANES), blk), pl.BlockSpec((1, T, d_inner), blk),
                pl.BlockSpec((1, T, d_att), blk),
                _const_spec(p["conv_w"].shape), _const_spec((1, cdim)), _const_spec((1, LANES)),
                _const_spec((1, d_inner)), _const_spec((1, d_inner)), _const_spec((LANES, d_inner)), _const_spec((T, T)),
                _const_spec((3, LANES, d_att))]
    out_shape = [jax.ShapeDtypeStruct((n, L, d_inner), BF16), jax.ShapeDtypeStruct((n, n_pairs, L, 2 * LANES), BF16),
                 jax.ShapeDtypeStruct((n, d_inner, d_state), F32)]
    out_specs = [pl.BlockSpec((1, T, d_inner), blk), pl.BlockSpec((1, n_pairs, T, 2 * LANES), lambda b, c: (b, 0, c, 0)),
                 pl.BlockSpec((1, d_inner, d_state), lambda b, c: (b, 0, 0))]
    scratch = [pltpu.VMEM((T + SUBLANES, cdim), F32), pltpu.VMEM((d_state, d_inner), F32),
               pltpu.VMEM((SUBLANES, LANES), F32), pltpu.VMEM((T, d_inner), F32)]
    return pl.pallas_call(
        functools.partial(_ssd_prompt_kernel, n_h, p_dim, d_state, n_att),
        grid=(n, L // T), in_specs=in_specs, out_specs=out_specs, out_shape=out_shape, scratch_shapes=scratch,
        compiler_params=_params(("parallel", "arbitrary")), name="ssd_prompt",
    )(xbc, dtf, z, kb, p["conv_w"], p["conv_b"], p["a_log"], p["d_skip_e"], p["norm_w"], p["e_heads"], tri, sel)


def _fox_prompt_kernel(hd, tk, per_iter, qt_ref, kx_ref, vt_ref, o_ref, qx, m_s, acc_s):
    i = pl.program_id(2)
    tq = qt_ref.shape[1]
    ratio = tq // tk
    n_ones = acc_s.shape[1] - hd
    cw = min(tq, FOX_COL_GROUP)
    n_cg = tq // cw

    qt = qt_ref[...].astype(F32)
    row = lax.broadcasted_iota(jnp.int32, qt.shape, 0)
    for hh in range(2):
        qm = jnp.where((row >= hh * hd) & (row < (hh + 1) * hd), qt, 0.0)
        sel = jnp.where((row >= 3 * hh) & (row < 3 * hh + 3), 1.0, 0.0)
        qx[hh] = jnp.concatenate([qm, sel], axis=0).astype(BF16)
    m_s[...] = jnp.full(m_s.shape, NEG_BIG, F32)
    acc_s[...] = jnp.zeros(acc_s.shape, F32)
    ones = jnp.ones((n_ones, tk), BF16)

    chains = [(hh, slice(cg * cw, (cg + 1) * cw), cg) for hh in range(2) for cg in range(n_cg)]

    def visibility(rel, cg):
        if rel is None:
            return "all"
        k_lo, k_hi = rel * tk, (rel + 1) * tk - 1
        q_lo, q_hi = cg * cw, (cg + 1) * cw - 1
        return "none" if k_lo > q_hi else ("all" if k_hi <= q_lo else "part")

    def blocks(js):
        kxs, vexts = [], []
        for j, _ in js:
            off = pl.multiple_of(j * tk, tk)
            kxs.append(kx_ref[pl.ds(off, tk), :])
            vtj = vt_ref[:, pl.ds(off, tk)]
            vexts.append([jnp.concatenate([vtj[hh * hd:(hh + 1) * hd, :], ones], axis=0) for hh in range(2)])
        live = [[visibility(rel, cg) for _, _, cg in chains] for _, rel in js]
        scores = [[_dot(kxs[b], qx[hh, :, cols]) if live[b][c] != "none" else None
                   for c, (hh, cols, _) in enumerate(chains)] for b in range(len(js))]
        probs = []
        for c, (hh, cols, cg) in enumerate(chains):
            m_run = m_s[hh, :, cols]
            steps = []
            for b, (_, rel) in enumerate(js):
                if live[b][c] == "none":
                    continue
                s = scores[b][c]
                if live[b][c] == "part":
                    key_i = lax.broadcasted_iota(jnp.int32, (tk, cw), 0) + rel * tk
                    qry_i = lax.broadcasted_iota(jnp.int32, (tk, cw), 1) + cg * cw
                    s = jnp.where(key_i <= qry_i, s, NEG_BIG)
                m_new = jnp.maximum(m_run, jnp.max(s, axis=0, keepdims=True))
                steps.append((b, jnp.exp2(m_run - m_new), jnp.exp2(s - m_new).astype(BF16)))
                m_run = m_new
            m_s[hh, :, cols] = m_run
            probs.append(steps)
        for c, (hh, cols, _) in enumerate(chains):
            acc = acc_s[hh, :, cols]
            for b, alpha, pm in probs[c]:
                acc = alpha * acc + _dot(vexts[b][hh], pm)
            acc_s[hh, :, cols] = acc

    def body(jb, carry):
        blocks([(jb * per_iter + b, None) for b in range(per_iter)])
        return carry

    n_full = i * ratio
    lax.fori_loop(0, n_full // per_iter, body, 0)
    rem = n_full % per_iter
    for rv in range(per_iter):
        if ratio % per_iter == 0 and rv:
            continue

        @pl.when(rem == rv)
        def _(rv=rv):
            blocks([(n_full - rv + b, None) for b in range(rv)] + [(n_full + b, b) for b in range(ratio)])

    outs = [acc_s[hh][:hd, :] / acc_s[hh][hd:hd + 1, :] for hh in range(2)]
    o_ref[...] = jnp.concatenate(outs, axis=0).astype(o_ref.dtype)


def _fox_prompt(qt, kx, vt, hd, blk, tk, per_iter):
    n, da, L = qt.shape
    n_pairs = da // (2 * hd)
    t = min(blk, L)
    tk = min(tk, t)
    nb = L // t
    in_specs = [pl.BlockSpec((None, 2 * hd, t), lambda b, p, i: (b, p, i)),
                pl.BlockSpec((None, None, L, 4 * hd), lambda b, p, i: (b, p, 0, 0)),
                pl.BlockSpec((None, 2 * hd, L), lambda b, p, i: (b, p, 0))]
    out_specs = pl.BlockSpec((None, 2 * hd, t), lambda b, p, i: (b, p, i))
    scratch = [pltpu.VMEM((2, 4 * hd, t), BF16), pltpu.VMEM((2, 1, t), F32), pltpu.VMEM((2, hd + 2 * SUBLANES, t), F32)]
    return pl.pallas_call(
        functools.partial(_fox_prompt_kernel, hd, tk, per_iter),
        grid=(n, n_pairs, nb), in_specs=in_specs, out_specs=out_specs,
        out_shape=jax.ShapeDtypeStruct((n, da, L), BF16), scratch_shapes=scratch,
        compiler_params=_params(("parallel", "parallel", "arbitrary")), name="fox_prompt",
    )(qt, kx, vt)


def _merge_kernel(alpha, att_transposed, x_ref, ys_ref, att_ref, g_ref, wso, wao, wo, gb, lg, lb, h_o):
    d = x_ref.shape[1]
    a = _dot(ys_ref[...], wso[...])
    if att_transposed:
        att = att_ref[0].astype(F32).T.astype(BF16)
    else:
        att = att_ref[...]
    b = _dot(att, wao[...])
    gt = _sigmoid(g_ref[...].astype(F32) + gb[...])
    merged = gt[:, :d] * a + gt[:, d:] * b
    o = _dot(merged.astype(BF16), wo[...])
    h_o[...] = _layer_norm(alpha * x_ref[...] + o, lg[...], lb[...])


def _merge(x2d, ys, att, gates, p, alpha, tm):
    m, d = x2d.shape
    att_transposed = att.ndim == 3
    tm = min(tm, att.shape[2] if att_transposed else m)
    row = lambda i: (i, 0)
    ins = [x2d, ys, att, gates]
    consts = [p["w_ssm_o"], p["w_att_o"], p["w_o"], p["gate_bias"], p["ln1_g"], p["ln1_b"]]
    in_specs = [pl.BlockSpec((tm, a.shape[1]), row) for a in ins] + [_const_spec(c.shape) for c in consts]
    if att_transposed:
        tps = att.shape[2] // tm
        in_specs[2] = pl.BlockSpec((1, att.shape[1], tm), lambda i: (i // tps, 0, i % tps))
    return pl.pallas_call(
        functools.partial(_merge_kernel, alpha, att_transposed),
        grid=(m // tm,), in_specs=in_specs, out_specs=pl.BlockSpec((tm, d), row),
        out_shape=jax.ShapeDtypeStruct((m, d), F32),
        compiler_params=_params(("parallel",)), name="merge",
    )(*ins, *consts)


def _ffn_kernel(alpha, shift, n_col_chunks, h_ref, prev_ref, wup, cw, cb, wdn, lg, lb, o_ref, tail_o, ext, act):
    c = pl.program_id(1)
    tm = h_ref.shape[0] * h_ref.shape[1]
    dff = cw.shape[1]
    width = cw.shape[0]
    r = ext.shape[0] - tm
    cwid = dff // n_col_chunks

    @pl.when(c == 0)
    def _():
        ext[0:r, :] = prev_ref[...].reshape(r, dff)

    h = h_ref[...].reshape(tm, h_ref.shape[2])
    hb = h.astype(BF16)
    for ch in range(n_col_chunks):
        lo = ch * cwid
        ext[r:r + tm, lo:lo + cwid] = _dot(hb, wup[:, lo:lo + cwid])
        val = _dot(hb, wup[:, dff + lo:dff + lo + cwid])
        conv = cb[:, lo:lo + cwid] + ext[r:r + tm, lo:lo + cwid] * cw[width - 1:width, lo:lo + cwid]
        for j in range(1, width):
            conv = conv + ext[r - j * shift:r - j * shift + tm, lo:lo + cwid] * cw[width - 1 - j:width - j, lo:lo + cwid]
        gelu = 0.5 * conv * (1.0 + lax.erf(conv * math.sqrt(0.5)))
        act[:, lo:lo + cwid] = (gelu * val).astype(BF16)
    tail = ext[tm:tm + r, :]
    ext[0:r, :] = tail
    tail_o[...] = tail.reshape(tail_o.shape)
    o_ref[...] = _layer_norm(alpha * h + _dot(act[...], wdn[...]), lg[...], lb[...]).reshape(o_ref.shape)


def _ffn(h3, prev, p, alpha, time_major, tile):
    d = h3.shape[2]
    dff = p["ffn_conv_w"].shape[1]
    if time_major:
        nt, n_s, _ = h3.shape
        tile = min(tile, n_s)
        hblk, pblk = (nt, tile, d), (prev.shape[0], tile, dff)
        hmap = pmap = lambda b, c: (0, b, 0)
        grid, shift = (n_s // tile, 1), tile
    else:
        n, L, _ = h3.shape
        tile = min(tile, L)
        hblk, pblk = (1, tile, d), (1, prev.shape[1], dff)
        hmap, pmap = (lambda b, c: (b, c, 0)), (lambda b, c: (b, 0, 0))
        grid, shift = (n, L // tile), 1
    tm, r = hblk[0] * hblk[1], pblk[0] * pblk[1]
    consts = [p["w_ffn_up"], p["ffn_conv_w"], p["ffn_conv_b"], p["w_ffn_down"], p["ln2_g"], p["ln2_b"]]
    in_specs = [pl.BlockSpec(hblk, hmap), pl.BlockSpec(pblk, pmap)] + [_const_spec(cst.shape) for cst in consts]
    out_shape = [jax.ShapeDtypeStruct(h3.shape, F32), jax.ShapeDtypeStruct(prev.shape, F32)]
    out_specs = [pl.BlockSpec(hblk, hmap), pl.BlockSpec(pblk, pmap)]
    scratch = [pltpu.VMEM((tm + r, dff), F32), pltpu.VMEM((tm, dff), BF16)]
    return pl.pallas_call(
        functools.partial(_ffn_kernel, alpha, shift, 2),
        grid=grid, in_specs=in_specs, out_specs=out_specs, out_shape=out_shape, scratch_shapes=scratch,
        compiler_params=_params(("parallel", "arbitrary")), name="ffn",
    )(h3, prev, *consts)


def _ssd_sample_kernel(n_h, p_dim, d_state,
                       xbc_ref, prev_ref, dtf_ref, z_ref, st_ref, cw_ref, cb_ref, alog_ref, dskip_ref, nw_ref, e_ref,
                       y_o, st_o, xw_s, b_s, c_s, dec_s, yoff_s):
    nt, sb, cdim = xbc_ref.shape
    width = cw_ref.shape[0]
    d_inner = n_h * p_dim
    gw = N_GROUPS * d_state
    hpg = n_h // N_GROUPS
    gl = hpg * p_dim
    rows = nt * sb

    a_row = -jnp.exp(alog_ref[...])
    xin = [prev_ref[j] for j in range(width - 1)] + [xbc_ref[t] for t in range(nt)]
    xs, bm, cm, dts, acs = [], [], [], [], []
    run = None
    for t in range(nt):
        conv = cb_ref[...]
        for j in range(width):
            conv = conv + xin[t + j] * cw_ref[j:j + 1, :]
        act = _silu(conv)
        xs.append(act[:, :d_inner])
        bm.append(act[:, d_inner:d_inner + gw])
        cm.append(act[:, d_inner + gw:d_inner + 2 * gw])
        dtf = dtf_ref[t]
        lane = lax.broadcasted_iota(jnp.int32, dtf.shape, 1)
        dt = jnp.where(lane < n_h, dtf, 0.0)
        run = dt * a_row if run is None else run + dt * a_row
        dts.append(dt)
        acs.append(run)
    stacked = jnp.concatenate(dts + acs, axis=0)
    exp_all = _dot3_l(stacked, e_ref[...])
    dt_e = [exp_all[t * sb:(t + 1) * sb] for t in range(nt)]
    acs_e = [exp_all[(nt + t) * sb:(nt + t + 1) * sb] for t in range(nt)]
    xdt = [xs[t] * dt_e[t] for t in range(nt)]

    cbf = [cm[t].astype(BF16).astype(F32) for t in range(nt)]
    bbf = [bm[t].astype(BF16).astype(F32) for t in range(nt)]
    y_diag = []
    for t in range(nt):
        acc = jnp.zeros((sb, d_inner), F32)
        for s in range(t + 1):
            parts = []
            for g in range(N_GROUPS):
                dotg = jnp.sum(cbf[t][:, g * d_state:(g + 1) * d_state] * bbf[s][:, g * d_state:(g + 1) * d_state],
                               axis=-1, keepdims=True)
                w = jnp.exp(acs_e[t][:, g * gl:(g + 1) * gl] - acs_e[s][:, g * gl:(g + 1) * gl])
                parts.append(dotg * w * xdt[s][:, g * gl:(g + 1) * gl])
            acc = acc + jnp.concatenate(parts, axis=1)
        y_diag.append(acc)

    pad = xw_s.shape[0] - rows
    for t in range(nt):
        xw_s[t * sb:(t + 1) * sb, :] = xdt[t] * jnp.exp(acs_e[nt - 1] - acs_e[t])
        b_s[t * sb:(t + 1) * sb, :] = bm[t]
        c_s[t * sb:(t + 1) * sb, :] = cm[t]
    if pad:
        xw_s[rows:, :] = jnp.zeros((pad, d_inner), F32)
        b_s[rows:, :] = jnp.zeros((pad, gw), F32)
        c_s[rows:, :] = jnp.zeros((pad, gw), F32)
    dec_s[...] = jnp.exp(acs[nt - 1])
    yoff_s[...] = jnp.zeros(yoff_s.shape, F32)
    xw_t = xw_s[...].T.astype(BF16)
    rp = xw_s.shape[0]
    row_id = lax.broadcasted_iota(jnp.int32, (rp, 1), 0)

    def per_seq(j, carry):
        sel = (row_id % sb) == j
        h0 = st_ref[j]
        h0b = h0.astype(BF16)
        dec = dec_s[pl.ds(j, 1), :]
        bsel = jnp.where(sel, b_s[...], 0.0).astype(BF16)
        csel = jnp.where(sel, c_s[...], 0.0).astype(BF16)
        for g in range(N_GROUPS):
            yo = _dot_nt(csel[:, g * d_state:(g + 1) * d_state], h0b[g * gl:(g + 1) * gl, :])
            yoff_s[:, g * gl:(g + 1) * gl] = yoff_s[:, g * gl:(g + 1) * gl] + yo
            upd = _dot(xw_t[g * gl:(g + 1) * gl, :], bsel[:, g * d_state:(g + 1) * d_state])
            for r in range(hpg):
                h = g * hpg + r
                lo = h * p_dim
                dcol = jnp.broadcast_to(dec[:, h:h + 1], (p_dim, d_state))
                st_o[j, lo:lo + p_dim, :] = h0[lo:lo + p_dim, :] * dcol + upd[r * p_dim:(r + 1) * p_dim, :]
        return carry

    lax.fori_loop(0, sb, per_seq, 0)

    for t in range(nt):
        y = y_diag[t] + yoff_s[t * sb:(t + 1) * sb, :] * jnp.exp(acs_e[t]) + xs[t] * dskip_ref[...]
        y = y * _silu(z_ref[t].astype(F32))
        outs = []
        for g in range(N_GROUPS):
            yg = y[:, g * gl:(g + 1) * gl]
            outs.append(yg * lax.rsqrt(jnp.mean(yg * yg, axis=-1, keepdims=True) + RMS_EPS))
        y_o[t] = (jnp.concatenate(outs, axis=1) * nw_ref[...]).astype(y_o.dtype)


def _ssd_sample(xbc_t, prev_t, dtf_t, z_t, state, p, n_h, p_dim, d_state, sb):
    nt, s, cdim = xbc_t.shape
    sb = min(sb, s)
    d_inner = n_h * p_dim
    gw = N_GROUPS * d_state
    rows_pad = max(LANES, -(-nt * sb // LANES) * LANES)
    tb = lambda i: (0, i, 0)
    in_specs = [pl.BlockSpec((nt, sb, cdim), tb), pl.BlockSpec((prev_t.shape[0], sb, cdim), tb),
                pl.BlockSpec((nt, sb, LANES), tb), pl.BlockSpec((nt, sb, d_inner), tb),
                pl.BlockSpec((sb, d_inner, d_state), lambda i: (i, 0, 0)),
                _const_spec(p["conv_w"].shape), _const_spec((1, cdim)), _const_spec((1, LANES)),
                _const_spec((1, d_inner)), _const_spec((1, d_inner)), _const_spec((LANES, d_inner))]
    out_shape = [jax.ShapeDtypeStruct((nt, s, d_inner), BF16), jax.ShapeDtypeStruct(state.shape, F32)]
    out_specs = [pl.BlockSpec((nt, sb, d_inner), tb), pl.BlockSpec((sb, d_inner, d_state), lambda i: (i, 0, 0))]
    scratch = [pltpu.VMEM((rows_pad, d_inner), F32), pltpu.VMEM((rows_pad, gw), F32), pltpu.VMEM((rows_pad, gw), F32),
               pltpu.VMEM((sb, LANES), F32), pltpu.VMEM((rows_pad, d_inner), F32)]
    return pl.pallas_call(
        functools.partial(_ssd_sample_kernel, n_h, p_dim, d_state),
        grid=(s // sb,), in_specs=in_specs, out_specs=out_specs, out_shape=out_shape, scratch_shapes=scratch,
        compiler_params=_params(("parallel",)), name="ssd_sample",
    )(xbc_t, prev_t, dtf_t, z_t, state, p["conv_w"], p["conv_b"], p["a_log"], p["d_skip_e"], p["norm_w"], p["e_heads"])


def _fox_sample_kernel(n_pages, ch, page, n_heads, hd, nt,
                       pt_ref, qbd_ref, kn_ref, vn_ref, lfn_ref, su_ref, k_hbm, v_hbm, lf_hbm,
                       o_ref, kbuf, vbuf, lfbuf, ksem, vsem, lfsem):
    s = pl.program_id(0)
    ns = pl.num_programs(0)
    n_chunks = n_pages // ch
    rows = nt * n_heads
    da = n_heads * hd

    def kv_copies(seq, chunk, slot):
        cps = []
        for i in range(ch):
            pid = pt_ref[seq, chunk * ch + i]
            dst = pl.ds(i * page, page)
            cps.append(pltpu.make_async_copy(k_hbm.at[pid], kbuf.at[slot, :, dst], ksem.at[slot]))
            cps.append(pltpu.make_async_copy(v_hbm.at[pid], vbuf.at[slot, :, dst], vsem.at[slot]))
        return cps

    def lf_copies(seq, slot):
        return [pltpu.make_async_copy(lf_hbm.at[pt_ref[seq, i]], lfbuf.at[slot, i], lfsem.at[slot]) for i in range(n_pages)]

    @pl.when(s == 0)
    def _():
        for cp in lf_copies(0, 0):
            cp.start()
        for cp in kv_copies(0, 0, 0):
            cp.start()

    ls = s % 2
    for cp in lf_copies(s, ls):
        cp.wait()

    @pl.when(s + 1 < ns)
    def _():
        for cp in lf_copies(s + 1, 1 - ls):
            cp.start()

    lf = lfbuf[ls]
    lf2 = lf.reshape(n_pages * n_heads, page)
    within = _dot3_l(lf2, su_ref[...]).reshape(n_pages, n_heads, page)
    tot = jnp.sum(lf, axis=-1, keepdims=True)
    bias_pages = [None] * n_pages
    run = jnp.zeros((n_heads, 1), F32)
    for i in reversed(range(n_pages)):
        bias_pages[i] = within[i] + run
        run = run + tot[i]

    qbd = qbd_ref[0]
    m_run = jnp.full((rows, 1), NEG_BIG, F32)
    l_run = jnp.zeros((rows, 1), F32)
    acc = jnp.zeros((rows, da), F32)
    for c in range(n_chunks):
        g = s * n_chunks + c
        slot = g % 2
        if c + 1 < n_chunks:
            for cp in kv_copies(s, c + 1, 1 - slot):
                cp.start()
        else:
            @pl.when(s + 1 < ns)
            def _():
                for cp in kv_copies(s + 1, 0, 1 - slot):
                    cp.start()
        for cp in kv_copies(s, c, slot):
            cp.wait()
        kc = kbuf[slot].astype(BF16)
        vc = vbuf[slot].astype(BF16)
        bias = jnp.concatenate(bias_pages[c * ch:(c + 1) * ch], axis=1)
        u = jnp.concatenate([bias] * nt, axis=0) + _dot(qbd, kc)
        m_new = jnp.maximum(m_run, jnp.max(u, axis=-1, keepdims=True))
        alpha = jnp.exp(m_run - m_new)
        pm = jnp.exp(u - m_new)
        l_run = alpha * l_run + jnp.sum(pm, axis=-1, keepdims=True)
        acc = alpha * acc + _dot_nt(pm.astype(BF16), vc)
        m_run = m_new

    lfn = lfn_ref[0]
    lane = lax.broadcasted_iota(jnp.int32, lfn.shape, 1)
    cn = jnp.zeros(lfn.shape, F32)
    for t in range(nt):
        cn = cn + jnp.where(lane >= t, lfn[:, t:t + 1], 0.0)
    n_slots = kn_ref.shape[1]
    nbias = jnp.concatenate([-cn[:, :n_slots]] * nt, axis=0)
    row_t = lax.broadcasted_iota(jnp.int32, (rows, n_slots), 0) // n_heads
    lane_r = lax.broadcasted_iota(jnp.int32, (rows, n_slots), 1)
    u = jnp.where(lane_r <= row_t, nbias + _dot_nt(qbd, kn_ref[0]), NEG_BIG)
    m_new = jnp.maximum(m_run, jnp.max(u, axis=-1, keepdims=True))
    alpha = jnp.exp(m_run - m_new)
    pm = jnp.exp(u - m_new)
    l_run = alpha * l_run + jnp.sum(pm, axis=-1, keepdims=True)
    acc = alpha * acc + _dot(pm.astype(BF16), vn_ref[0])
    out = acc / l_run
    col_h = lax.broadcasted_iota(jnp.int32, (rows, da), 1) // hd
    row_h = lax.broadcasted_iota(jnp.int32, (rows, da), 0) % n_heads
    out = jnp.where(col_h == row_h, out, 0.0)
    o_ref[0] = jnp.concatenate(
        [jnp.sum(out[t * n_heads:(t + 1) * n_heads], axis=0, keepdims=True) for t in range(nt)]
        + [jnp.zeros((SUBLANES - nt, da), F32)], axis=0).astype(o_ref.dtype)


def _fox_sample(page_table, qbd, k_new, v_new, lf_new, cache_k, cache_v, cache_lft, n_heads, hd, nt, ch):
    s, n_pages = page_table.shape
    page = cache_k.shape[2]
    da = n_heads * hd
    ch = min(ch, n_pages)
    while n_pages % ch:
        ch -= 1
    rows = nt * n_heads
    su = jnp.triu(jnp.ones((page, page), F32), k=1).T.astype(BF16)
    per_seq = lambda i, pt: (i, 0, 0)
    grid_spec = pltpu.PrefetchScalarGridSpec(
        num_scalar_prefetch=1, grid=(s,),
        in_specs=[pl.BlockSpec((1, rows, da), per_seq), pl.BlockSpec((1, k_new.shape[1], da), per_seq),
                  pl.BlockSpec((1, k_new.shape[1], da), per_seq), pl.BlockSpec((1, n_heads, LANES), per_seq),
                  pl.BlockSpec((page, page), lambda i, pt: (0, 0)),
                  pl.BlockSpec(memory_space=pl.ANY), pl.BlockSpec(memory_space=pl.ANY), pl.BlockSpec(memory_space=pl.ANY)],
        out_specs=pl.BlockSpec((1, SUBLANES, da), per_seq),
        scratch_shapes=[pltpu.VMEM((2, da, ch * page), F32), pltpu.VMEM((2, da, ch * page), F32),
                        pltpu.VMEM((2, n_pages, n_heads, page), F32),
                        pltpu.SemaphoreType.DMA((2,)), pltpu.SemaphoreType.DMA((2,)), pltpu.SemaphoreType.DMA((2,))])
    return pl.pallas_call(
        functools.partial(_fox_sample_kernel, n_pages, ch, page, n_heads, hd, nt),
        grid_spec=grid_spec, out_shape=jax.ShapeDtypeStruct((s, SUBLANES, da), BF16),
        compiler_params=_params(("arbitrary",)), name="fox_sample",
    )(page_table, qbd, k_new, v_new, lf_new, su, cache_k, cache_v, cache_lft)


def _prep_params(dims, w_in, conv_ssm_w, conv_ssm_b, dt_bias, a_log, d_skip, ssm_norm_w, f_bias, gate_bias,
                 w_ssm_o, w_att_o, w_o, ln1_g, ln1_b, w_ffn_up, ffn_conv_w, ffn_conv_b, w_ffn_down, ln2_g, ln2_b):
    d_inner, cdim, n_h, d_att, n_att, d_model, p_dim, hd = dims
    z_end = d_inner
    xbc_end = z_end + cdim
    dt_end = xbc_end + n_h
    q_end = dt_end + d_att
    k_end = q_end + d_att
    v_end = k_end + d_att
    f_end = v_end + n_att
    row = lambda a: a.reshape(1, -1).astype(F32)
    pad_lanes = lambda a: jnp.pad(a, ((0, 0), (0, LANES - a.shape[1])))
    scale = hd ** -0.5
    w = {
        "n_dt": n_h, "n_att": n_att,
        "z": w_in[:, :z_end].astype(BF16),
        "xbc": w_in[:, z_end:xbc_end].astype(BF16),
        "q": (w_in[:, dt_end:q_end] * scale).astype(BF16),
        "q_log2": (w_in[:, dt_end:q_end] * (scale * LOG2E)).astype(BF16),
        "k": w_in[:, q_end:k_end].astype(BF16),
        "v": w_in[:, k_end:v_end].astype(BF16),
        "g": w_in[:, f_end:].astype(BF16),
        "sm": pad_lanes(jnp.concatenate([w_in[:, xbc_end:dt_end], w_in[:, v_end:f_end]], axis=1)).astype(BF16),
        "bsm": pad_lanes(jnp.concatenate([row(dt_bias), row(f_bias)], axis=1)),
    }
    head_of_lane = jnp.arange(d_inner) // p_dim
    p = {
        "conv_w": conv_ssm_w.astype(F32), "conv_b": row(conv_ssm_b),
        "a_log": pad_lanes(row(a_log)),
        "d_skip_e": row(jnp.repeat(d_skip, p_dim)), "norm_w": row(ssm_norm_w),
        "e_heads": (jnp.arange(LANES)[:, None] == head_of_lane[None, :]).astype(BF16),
        "w_ssm_o": w_ssm_o.astype(BF16), "w_att_o": w_att_o.astype(BF16), "w_o": w_o.astype(BF16),
        "gate_bias": row(gate_bias), "ln1_g": row(ln1_g), "ln1_b": row(ln1_b),
        "w_ffn_up": w_ffn_up.astype(BF16), "ffn_conv_w": ffn_conv_w.astype(F32), "ffn_conv_b": row(ffn_conv_b),
        "w_ffn_down": w_ffn_down.astype(BF16), "ln2_g": row(ln2_g), "ln2_b": row(ln2_b),
    }
    return w, p


def _forward(x_prompt, x_sample, cache_k, cache_v, cache_logf, state_conv_ssm, state_ssm, state_conv_ffn, page_table,
             w_in, conv_ssm_w, conv_ssm_b, dt_bias, a_log, d_skip, ssm_norm_w, f_bias, gate_bias, w_ssm_o, w_att_o,
             w_o, ln1_g, ln1_b, w_ffn_up, ffn_conv_w, ffn_conv_b, w_ffn_down, ln2_g, ln2_b,
             tm_proj=256, tm_merge=512, tm_ffn=256, att_blk=1024, att_tk=512, att_per_iter=1, sb=8, sbf=32, ch=16):
    depth = w_in.shape[0]
    assert depth == 1, "single-layer step"
    n_p, L, d_model = x_prompt.shape
    n_s, nt, _ = x_sample.shape
    n_pool, page, n_att, hd = cache_k.shape[1:]
    n_h, p_dim, d_state = state_ssm.shape[2:]
    d_inner = n_h * p_dim
    cdim = conv_ssm_w.shape[2]
    d_att = n_att * hd
    dff = ffn_conv_w.shape[2]
    ssm_w = conv_ssm_w.shape[1]
    ffn_w = ffn_conv_w.shape[1]
    alpha = (2.0 * depth) ** 0.25
    dims = (d_inner, cdim, n_h, d_att, n_att, d_model, p_dim, hd)
    lyr = 0
    w, p = _prep_params(dims, w_in[lyr], conv_ssm_w[lyr], conv_ssm_b[lyr], dt_bias[lyr], a_log[lyr], d_skip[lyr],
                        ssm_norm_w[lyr], f_bias[lyr], gate_bias[lyr], w_ssm_o[lyr], w_att_o[lyr], w_o[lyr], ln1_g[lyr],
                        ln1_b[lyr], w_ffn_up[lyr], ffn_conv_w[lyr], ffn_conv_b[lyr], w_ffn_down[lyr], ln2_g[lyr], ln2_b[lyr])

    xp2 = x_prompt.reshape(n_p * L, d_model)
    wp = dict(w, q=w["q_log2"])
    z, xbc, _, qt, kb, kt, _, vt, vtb, gates, dtf, lft = _in_proj(x_prompt, wp, tm_proj)
    xbc3 = xbc.reshape(n_p, L, cdim)
    y_ssm, kx, h_fin = _ssd_prompt(xbc3, dtf.reshape(n_p, L, LANES), z.reshape(n_p, L, d_inner), kb.reshape(n_p, L, d_att),
                                   p, n_h, p_dim, d_state, n_att)
    att_t = _fox_prompt(qt, kx, vtb, hd, att_blk, att_tk, att_per_iter)
    h1 = _merge(xp2, y_ssm.reshape(n_p * L, d_inner), att_t, gates, p, alpha, tm_merge)
    prev0 = jnp.zeros((n_p, SUBLANES, dff), F32)
    y_p, tail_p = _ffn(h1.reshape(n_p, L, d_model), prev0, p, alpha, False, tm_ffn)
    k_p = jnp.transpose(kt.reshape(1, n_p, n_att, hd, L), (0, 1, 4, 2, 3))
    v_p = jnp.transpose(vt.reshape(1, n_p, n_att, hd, L), (0, 1, 4, 2, 3))
    lf_p = jnp.transpose(lft.reshape(1, n_p, n_att, L), (0, 1, 3, 2))
    cs_p = xbc3[:, L - (ssm_w - 1):, :][None]
    ss_p = h_fin.reshape(1, n_p, n_h, p_dim, d_state)
    cf_p = tail_p[:, SUBLANES - (ffn_w - 1):, :][None]

    xs3 = jnp.transpose(x_sample, (1, 0, 2))
    xs2 = xs3.reshape(nt * n_s, d_model)
    z, xbc, q, _, kb, kt, vb, vt, _, gates, dtf, lft = _in_proj(xs3, w, tm_proj)
    xbc_t = xbc.reshape(nt, n_s, cdim)
    prev_t = jnp.transpose(state_conv_ssm[lyr], (1, 0, 2))
    y_ssm, ss_new = _ssd_sample(xbc_t, prev_t, dtf.reshape(nt, n_s, LANES), z.reshape(nt, n_s, d_inner),
                                state_ssm[lyr].reshape(n_s, d_inner, d_state), p, n_h, p_dim, d_state, sb)
    q_t = q.reshape(nt, n_s, n_att, hd)
    eye = jnp.eye(n_att, dtype=BF16)
    qbd = jnp.einsum("tshe,hg->sthge", q_t, eye).reshape(n_s, nt * n_att, d_att)
    seq_major = lambda a: jnp.pad(jnp.transpose(a.reshape(nt, n_s, d_att), (1, 0, 2)), ((0, 0), (0, NEW_KEY_SLOTS - nt), (0, 0)))
    lf_new = jnp.transpose(dtf[:, n_h:n_h + n_att].reshape(nt, n_s, n_att), (1, 2, 0))
    lf_new = jnp.pad(lf_new, ((0, 0), (0, 0), (0, LANES - nt)))
    att = _fox_sample(page_table, qbd, seq_major(kb), seq_major(vb), lf_new,
                      jnp.transpose(cache_k[lyr], (0, 2, 3, 1)).reshape(n_pool, d_att, page),
                      jnp.transpose(cache_v[lyr], (0, 2, 3, 1)).reshape(n_pool, d_att, page),
                      jnp.transpose(cache_logf[lyr], (0, 2, 1)), n_att, hd, nt, ch)
    att_t = jnp.transpose(att[:, :nt, :], (1, 0, 2)).reshape(nt * n_s, d_att)
    h1 = _merge(xs2, y_ssm.reshape(nt * n_s, d_inner), att_t, gates, p, alpha, tm_merge)
    prev_f = jnp.transpose(state_conv_ffn[lyr], (1, 0, 2))
    y_s, tail_s = _ffn(h1.reshape(nt, n_s, d_model), prev_f, p, alpha, True, sbf)
    back = lambda a, *tail: jnp.transpose(a.reshape(nt, n_s, *tail), (1, 0) + tuple(range(2, 2 + len(tail))))
    y_s = back(y_s, d_model)
    k_s = jnp.transpose(kt.reshape(1, nt, n_att, hd, n_s), (0, 4, 1, 2, 3))
    v_s = jnp.transpose(vt.reshape(1, nt, n_att, hd, n_s), (0, 4, 1, 2, 3))
    lf_s = jnp.transpose(lft.reshape(1, nt, n_att, n_s), (0, 3, 1, 2))
    cs_s = jnp.transpose(jnp.concatenate([prev_t, xbc_t], axis=0)[-(ssm_w - 1):], (1, 0, 2))[None]
    ss_s = ss_new.reshape(1, n_s, n_h, p_dim, d_state)
    cf_s = jnp.transpose(tail_s, (1, 0, 2))[None]
    y_p = y_p.reshape(n_p, L, d_model)
    return (y_p, y_s, k_p, v_p, lf_p, cs_p, ss_p, cf_p, k_s, v_s, lf_s, cs_s, ss_s, cf_s)


def kernel(x_prompt, x_sample, cache_k, cache_v, cache_logf, state_conv_ssm, state_ssm, state_conv_ffn, page_table, w_in, conv_ssm_w, conv_ssm_b, dt_bias, a_log, d_skip, ssm_norm_w, f_bias, gate_bias, w_ssm_o, w_att_o, w_o, ln1_g, ln1_b, w_ffn_up, ffn_conv_w, ffn_conv_b, w_ffn_down, ln2_g, ln2_b):
    return _forward(x_prompt, x_sample, cache_k, cache_v, cache_logf, state_conv_ssm, state_ssm, state_conv_ffn, page_table,
                    w_in, conv_ssm_w, conv_ssm_b, dt_bias, a_log, d_skip, ssm_norm_w, f_bias, gate_bias, w_ssm_o, w_att_o,
                    w_o, ln1_g, ln1_b, w_ffn_up, ffn_conv_w, ffn_conv_b, w_ffn_down, ln2_g, ln2_b)
```

```python
import functools
import math

import jax
import jax.numpy as jnp
from jax import lax
from jax.experimental import pallas as pl
from jax.experimental.pallas import tpu as pltpu

F32 = jnp.float32
BF16 = jnp.bfloat16

LN_EPS = 1e-5
RMS_EPS = 1e-5
SSD_CHUNK = 128
N_GROUPS = 2
NEG_BIG = -1e30
LOG2E = math.log2(math.e)
V7X_VMEM_BYTES = 64 * 1024 * 1024
VMEM_LIMIT = V7X_VMEM_BYTES - 8 * 1024 * 1024
LANES = 128
SUBLANES = 8
NEW_KEY_SLOTS = 16
FOX_COL_GROUP = 512


def _dot(a, b):
    return jnp.dot(a, b, preferred_element_type=F32)


def _dot_nt(a, b):
    return lax.dot_general(a, b, (((1,), (1,)), ((), ())), preferred_element_type=F32)


def _split3(a):
    hi = a.astype(BF16)
    r1 = a - hi.astype(F32)
    mid = r1.astype(BF16)
    lo = (r1 - mid.astype(F32)).astype(BF16)
    return hi, mid, lo


def _dot3_l(a, m):
    hi, mid, lo = _split3(a)
    return _dot(hi, m) + _dot(mid, m) + _dot(lo, m)


def _dot3_r(m, a):
    hi, mid, lo = _split3(a)
    return _dot(m, hi) + _dot(m, mid) + _dot(m, lo)


def _silu(x):
    return x * (1.0 / (1.0 + jnp.exp(-x)))


def _sigmoid(x):
    return 1.0 / (1.0 + jnp.exp(-x))


def _layer_norm(x, g, b):
    mu = jnp.mean(x, axis=-1, keepdims=True)
    xc = x - mu
    var = jnp.mean(xc * xc, axis=-1, keepdims=True)
    return xc * lax.rsqrt(var + LN_EPS) * g + b


def _const_spec(shape):
    nd = len(shape)
    return pl.BlockSpec(shape, lambda *_: (0,) * nd, pipeline_mode=pl.Buffered(1))


def _params(sem):
    return pltpu.CompilerParams(dimension_semantics=sem, vmem_limit_bytes=VMEM_LIMIT)


def _in_proj_kernel(n_dt, n_att, x_ref, wz, wxbc, wq, wk, wv, wg, wsm, bsm,
                    z_o, xbc_o, q_o, qt_o, kb_o, kt_o, vb_o, vt_o, vtb_o, g_o, dtf_o, lft_o):
    xb = x_ref[...].astype(BF16)
    z_o[...] = _dot(xb, wz[...]).astype(BF16)
    xbc_o[...] = _dot(xb, wxbc[...])
    q = _dot(xb, wq[...])
    q_o[...] = q.astype(BF16)
    qt_o[0] = q.T.astype(BF16)
    k = _dot(xb, wk[...])
    kb_o[...] = k.astype(BF16)
    kt_o[0] = k.T
    v = _dot(xb, wv[...])
    vb_o[...] = v.astype(BF16)
    vt = v.T
    vt_o[0] = vt
    vtb_o[0] = vt.astype(BF16)
    g_o[...] = _dot(xb, wg[...]).astype(BF16)
    s = bsm[...] + _dot(xb, wsm[...])
    lane = lax.broadcasted_iota(jnp.int32, s.shape, 1)
    t = jnp.log1p(jnp.exp(-jnp.abs(s)))
    dtf = jnp.where(lane < n_dt, jnp.maximum(s, 0.0) + t, jnp.minimum(s, 0.0) - t)
    dtf_o[...] = dtf
    lft_o[0] = dtf.T[n_dt:n_dt + n_att, :]


def _in_proj(x3d, w, tm):
    nb, lb, d = x3d.shape
    m = nb * lb
    tm = min(tm, lb)
    tps = lb // tm
    n_att = w["n_att"]
    widths = [w[k].shape[1] for k in ("z", "xbc", "q", "k", "v", "g", "sm")]
    row = lambda i: (i, 0)
    tr = lambda i: (i // tps, 0, i % tps)
    in_specs = [pl.BlockSpec((tm, d), row)] + [_const_spec((d, wd)) for wd in widths] + [_const_spec((1, LANES))]
    da = w["k"].shape[1]
    outs = [("z", BF16, None), ("xbc", F32, None), ("q", BF16, None), ("q", BF16, da), ("k", BF16, None), ("k", F32, da),
            ("v", BF16, None), ("v", F32, da), ("v", BF16, da), ("g", BF16, None), ("sm", F32, None), ("sm", F32, n_att)]
    out_shape, out_specs = [], []
    for k, dt, trows in outs:
        if trows is None:
            out_shape.append(jax.ShapeDtypeStruct((m, w[k].shape[1]), dt))
            out_specs.append(pl.BlockSpec((tm, w[k].shape[1]), row))
        else:
            out_shape.append(jax.ShapeDtypeStruct((nb, trows, lb), dt))
            out_specs.append(pl.BlockSpec((1, trows, tm), tr))
    return pl.pallas_call(
        functools.partial(_in_proj_kernel, w["n_dt"], n_att),
        grid=(m // tm,), in_specs=in_specs, out_specs=out_specs, out_shape=out_shape,
        compiler_params=_params(("parallel",)), name="in_proj",
    )(x3d.reshape(m, d), w["z"], w["xbc"], w["q"], w["k"], w["v"], w["g"], w["sm"], w["bsm"])


def _ssd_prompt_kernel(n_h, p_dim, d_state, n_att,
                       xbc_ref, dtf_ref, z_ref, kb_ref, cw_ref, cb_ref, alog_ref, dskip_ref, nw_ref, e_ref, tri_ref, sel_ref,
                       y_o, kx_o, hout_o, ext, ht, carry, ysc):
    c = pl.program_id(1)
    nc = pl.num_programs(1)
    T = xbc_ref.shape[1]
    d_inner = n_h * p_dim
    gw = N_GROUPS * d_state
    hpg = n_h // N_GROUPS

    @pl.when(c == 0)
    def _():
        ext[0:SUBLANES, :] = jnp.zeros((SUBLANES, ext.shape[1]), F32)
        ht[...] = jnp.zeros(ht.shape, F32)
        carry[...] = jnp.zeros(carry.shape, F32)

    ext[SUBLANES:SUBLANES + T, :] = xbc_ref[0]
    width = cw_ref.shape[0]
    conv = cb_ref[...] + ext[SUBLANES:SUBLANES + T, :] * cw_ref[width - 1:width, :]
    for j in range(1, width):
        conv = conv + ext[SUBLANES - j:SUBLANES - j + T, :] * cw_ref[width - 1 - j:width - j, :]
    ext[0:SUBLANES, :] = ext[T:T + SUBLANES, :]
    act = _silu(conv)
    xs = act[:, :d_inner]
    bm = act[:, d_inner:d_inner + gw]
    cm = act[:, d_inner + gw:d_inner + 2 * gw]

    dtf = dtf_ref[0]
    lane = lax.broadcasted_iota(jnp.int32, dtf.shape, 1)
    a_row = -jnp.exp(alog_ref[...])
    dt = jnp.where(lane < n_h, dtf, 0.0)
    comb = jnp.where(lane < n_h, dtf * a_row, jnp.where(lane < n_h + n_att, dtf, 0.0))
    cs = _dot3_r(tri_ref[...], comb)
    cs_t = cs.T
    acs = jnp.where(lane < n_h, cs, 0.0)

    clf = jnp.where((lane >= n_h) & (lane < n_h + n_att), cs + carry[0:1, :], 0.0)
    carry[...] = jnp.broadcast_to(clf[T - 1:T, :], carry.shape)
    pieces = _split3(clf * (-LOG2E))
    ext_k = _dot(pieces[0], sel_ref[0]) + _dot(pieces[1], sel_ref[1]) + _dot(pieces[2], sel_ref[2])
    kb = kb_ref[0]
    for pr in range(kx_o.shape[1]):
        kx_o[0, pr] = jnp.concatenate([kb[:, pr * LANES:(pr + 1) * LANES],
                                       ext_k[:, pr * LANES:(pr + 1) * LANES].astype(BF16)], axis=1)

    e = e_ref[...]
    dt_e = _dot3_l(dt, e)
    acs_e = _dot3_l(acs, e)
    acs_last = acs_e[T - 1:T, :]
    xdt = xs * dt_e

    row_i = lax.broadcasted_iota(jnp.int32, (T, T), 0)
    col_i = lax.broadcasted_iota(jnp.int32, (T, T), 1)
    causal = col_i <= row_i
    half = lax.broadcasted_iota(jnp.int32, (T, 2 * p_dim), 1) < p_dim
    for g in range(N_GROUPS):
        cg = cm[:, g * d_state:(g + 1) * d_state].astype(BF16)
        bg = bm[:, g * d_state:(g + 1) * d_state].astype(BF16)
        cbm = _dot_nt(cg, bg)
        for pr in range(hpg // 2):
            h0 = g * hpg + 2 * pr
            ms = []
            for h in (h0, h0 + 1):
                seg = cs[:, h:h + 1] - cs_t[h:h + 1, :]
                dec = jnp.exp(jnp.where(causal, seg, NEG_BIG))
                ms.append((cbm * dec).astype(BF16))
            lo = h0 * p_dim
            xp = xdt[:, lo:lo + 2 * p_dim].astype(BF16)
            ysc[:, lo:lo + 2 * p_dim] = jnp.where(half, _dot(ms[0], xp), _dot(ms[1], xp))

    ht_old = ht[...]
    ht_b = ht_old.astype(BF16)
    xw = (xdt * jnp.exp(acs_last - acs_e)).astype(BF16)
    gl = hpg * p_dim
    y_off = []
    s_new = []
    for g in range(N_GROUPS):
        cg = cm[:, g * d_state:(g + 1) * d_state].astype(BF16)
        y_off.append(_dot(cg, ht_b[:, g * gl:(g + 1) * gl]))
        bgt = bm[:, g * d_state:(g + 1) * d_state].T.astype(BF16)
        s_new.append(_dot(bgt, xw[:, g * gl:(g + 1) * gl]))
    y_off = jnp.concatenate(y_off, axis=1) * jnp.exp(acs_e)
    ht_new = ht_old * jnp.exp(acs_last) + jnp.concatenate(s_new, axis=1)
    ht[...] = ht_new

    y = ysc[...] + y_off + xs * dskip_ref[...]
    y = y * _silu(z_ref[0].astype(F32))
    outs = []
    for g in range(N_GROUPS):
        yg = y[:, g * gl:(g + 1) * gl]
        outs.append(yg * lax.rsqrt(jnp.mean(yg * yg, axis=-1, keepdims=True) + RMS_EPS))
    y_o[0] = (jnp.concatenate(outs, axis=1) * nw_ref[...]).astype(y_o.dtype)

    @pl.when(c == nc - 1)
    def _():
        hout_o[0] = ht_new.T


def _ssd_prompt(xbc, dtf, z, kb, p, n_h, p_dim, d_state, n_att):
    n, L, cdim = xbc.shape
    d_att = kb.shape[2]
    n_pairs = d_att // LANES
    T = SSD_CHUNK if L % SSD_CHUNK == 0 else L
    d_inner = n_h * p_dim
    tri = jnp.tril(jnp.ones((T, T), F32)).astype(BF16)
    blk = lambda b, c: (b, c, 0)
    heads = jnp.arange(n_att)
    sel = jnp.zeros((3, LANES, d_att), F32)
    for piece in range(3):
        sel = sel.at[piece, n_h + heads, (heads // 2) * LANES + 3 * (heads % 2) + piece].set(1.0)
    sel = sel.astype(BF16)
    in_specs = [pl.BlockSpec((1, T, cdim), blk), pl.BlockSpec((1, T, LANES), blk), pl.BlockSpec((1, T, d_inner), blk),
                pl.BlockSpec((1, T, d_att), blk),
                _const_spec(p["conv_w"].shape), _const_spec((1, cdim)), _const_spec((1, LANES)),
                _const_spec((1, d_inner)), _const_spec((1, d_inner)), _const_spec((LANES, d_inner)), _const_spec((T, T)),
                _const_spec((3, LANES, d_att))]
    out_shape = [jax.ShapeDtypeStruct((n, L, d_inner), BF16), jax.ShapeDtypeStruct((n, n_pairs, L, 2 * LANES), BF16),
                 jax.ShapeDtypeStruct((n, d_inner, d_state), F32)]
    out_specs = [pl.BlockSpec((1, T, d_inner), blk), pl.BlockSpec((1, n_pairs, T, 2 * LANES), lambda b, c: (b, 0, c, 0)),
                 pl.BlockSpec((1, d_inner, d_state), lambda b, c: (b, 0, 0))]
    scratch = [pltpu.VMEM((T + SUBLANES, cdim), F32), pltpu.VMEM((d_state, d_inner), F32),
               pltpu.VMEM((SUBLANES, LANES), F32), pltpu.VMEM((T, d_inner), F32)]
    return pl.pallas_call(
        functools.partial(_ssd_prompt_kernel, n_h, p_dim, d_state, n_att),
        grid=(n, L // T), in_specs=in_specs, out_specs=out_specs, out_shape=out_shape, scratch_shapes=scratch,
        compiler_params=_params(("parallel", "arbitrary")), name="ssd_prompt",
    )(xbc, dtf, z, kb, p["conv_w"], p["conv_b"], p["a_log"], p["d_skip_e"], p["norm_w"], p["e_heads"], tri, sel)


def _fox_prompt_kernel(hd, tk, qt_ref, kx_ref, vt_ref, o_ref, qx, m_s, acc_s):
    i = pl.program_id(2)
    tq = qt_ref.shape[1]
    ratio = tq // tk
    n_ones = acc_s.shape[1] - hd
    cw = min(tq, FOX_COL_GROUP)
    n_cg = tq // cw

    qt = qt_ref[...].astype(F32)
    row = lax.broadcasted_iota(jnp.int32, qt.shape, 0)
    for hh in range(2):
        qm = jnp.where((row >= hh * hd) & (row < (hh + 1) * hd), qt, 0.0)
        sel = jnp.where((row >= 3 * hh) & (row < 3 * hh + 3), 1.0, 0.0)
        qx[hh] = jnp.concatenate([qm, sel], axis=0).astype(BF16)
    m_s[...] = jnp.full(m_s.shape, NEG_BIG, F32)
    acc_s[...] = jnp.zeros(acc_s.shape, F32)
    ones = jnp.ones((n_ones, tk), BF16)

    chains = [(hh, slice(cg * cw, (cg + 1) * cw), cg) for hh in range(2) for cg in range(n_cg)]

    def visibility(rel, cg):
        if rel is None:
            return "all"
        k_lo, k_hi = rel * tk, (rel + 1) * tk - 1
        q_lo, q_hi = cg * cw, (cg + 1) * cw - 1
        return "none" if k_lo > q_hi else ("all" if k_hi <= q_lo else "part")

    def blocks(js):
        kxs, vexts = [], []
        for j, _ in js:
            off = pl.multiple_of(j * tk, tk)
            kxs.append(kx_ref[pl.ds(off, tk), :])
            vtj = vt_ref[:, pl.ds(off, tk)]
            vexts.append([jnp.concatenate([vtj[hh * hd:(hh + 1) * hd, :], ones], axis=0) for hh in range(2)])
        live = [[visibility(rel, cg) for _, _, cg in chains] for _, rel in js]
        scores = [[_dot(kxs[b], qx[hh, :, cols]) if live[b][c] != "none" else None
                   for c, (hh, cols, _) in enumerate(chains)] for b in range(len(js))]
        probs = []
        for c, (hh, cols, cg) in enumerate(chains):
            m_run = m_s[hh, :, cols]
            steps = []
            for b, (_, rel) in enumerate(js):
                if live[b][c] == "none":
                    continue
                s = scores[b][c]
                if live[b][c] == "part":
                    key_i = lax.broadcasted_iota(jnp.int32, (tk, cw), 0) + rel * tk
                    qry_i = lax.broadcasted_iota(jnp.int32, (tk, cw), 1) + cg * cw
                    s = jnp.where(key_i <= qry_i, s, NEG_BIG)
                m_new = jnp.maximum(m_run, jnp.max(s, axis=0, keepdims=True))
                steps.append((b, jnp.exp2(m_run - m_new), jnp.exp2(s - m_new).astype(BF16)))
                m_run = m_new
            m_s[hh, :, cols] = m_run
            probs.append(steps)
        for c, (hh, cols, _) in enumerate(chains):
            acc = acc_s[hh, :, cols]
            for b, alpha, pm in probs[c]:
                acc = alpha * acc + _dot(vexts[b][hh], pm)
            acc_s[hh, :, cols] = acc

    def body(j, carry):
        blocks([(j, None)])
        return carry

    n_full = i * ratio
    lax.fori_loop(0, n_full, body, 0)
    blocks([(n_full + b, b) for b in range(ratio)])

    outs = [acc_s[hh][:hd, :] / acc_s[hh][hd:hd + 1, :] for hh in range(2)]
    o_ref[...] = jnp.concatenate(outs, axis=0).astype(o_ref.dtype)


def _fox_prompt(qt, kx, vt, hd, blk, tk):
    n, da, L = qt.shape
    n_pairs = da // (2 * hd)
    t = min(blk, L)
    tk = min(tk, t)
    nb = L // t
    in_specs = [pl.BlockSpec((None, 2 * hd, t), lambda b, p, i: (b, p, i)),
                pl.BlockSpec((None, None, L, 4 * hd), lambda b, p, i: (b, p, 0, 0)),
                pl.BlockSpec((None, 2 * hd, L), lambda b, p, i: (b, p, 0))]
    out_specs = pl.BlockSpec((None, 2 * hd, t), lambda b, p, i: (b, p, i))
    scratch = [pltpu.VMEM((2, 4 * hd, t), BF16), pltpu.VMEM((2, 1, t), F32), pltpu.VMEM((2, hd + 2 * SUBLANES, t), F32)]
    return pl.pallas_call(
        functools.partial(_fox_prompt_kernel, hd, tk),
        grid=(n, n_pairs, nb), in_specs=in_specs, out_specs=out_specs,
        out_shape=jax.ShapeDtypeStruct((n, da, L), BF16), scratch_shapes=scratch,
        compiler_params=_params(("parallel", "parallel", "arbitrary")), name="fox_prompt",
    )(qt, kx, vt)


def _merge_kernel(alpha, att_transposed, x_ref, ys_ref, att_ref, g_ref, wso, wao, wo, gb, lg, lb, h_o):
    d = x_ref.shape[1]
    a = _dot(ys_ref[...], wso[...])
    if att_transposed:
        att = att_ref[0].astype(F32).T.astype(BF16)
    else:
        att = att_ref[...]
    b = _dot(att, wao[...])
    gt = _sigmoid(g_ref[...].astype(F32) + gb[...])
    merged = gt[:, :d] * a + gt[:, d:] * b
    o = _dot(merged.astype(BF16), wo[...])
    h_o[...] = _layer_norm(alpha * x_ref[...] + o, lg[...], lb[...])


def _merge(x2d, ys, att, gates, p, alpha, tm):
    m, d = x2d.shape
    att_transposed = att.ndim == 3
    tm = min(tm, att.shape[2] if att_transposed else m)
    row = lambda i: (i, 0)
    ins = [x2d, ys, att, gates]
    consts = [p["w_ssm_o"], p["w_att_o"], p["w_o"], p["gate_bias"], p["ln1_g"], p["ln1_b"]]
    in_specs = [pl.BlockSpec((tm, a.shape[1]), row) for a in ins] + [_const_spec(c.shape) for c in consts]
    if att_transposed:
        tps = att.shape[2] // tm
        in_specs[2] = pl.BlockSpec((1, att.shape[1], tm), lambda i: (i // tps, 0, i % tps))
    return pl.pallas_call(
        functools.partial(_merge_kernel, alpha, att_transposed),
        grid=(m // tm,), in_specs=in_specs, out_specs=pl.BlockSpec((tm, d), row),
        out_shape=jax.ShapeDtypeStruct((m, d), F32),
        compiler_params=_params(("parallel",)), name="merge",
    )(*ins, *consts)


def _ffn_kernel(alpha, shift, n_col_chunks, h_ref, prev_ref, wup, cw, cb, wdn, lg, lb, o_ref, tail_o, ext, act):
    c = pl.program_id(1)
    tm = h_ref.shape[0] * h_ref.shape[1]
    dff = cw.shape[1]
    width = cw.shape[0]
    r = ext.shape[0] - tm
    cwid = dff // n_col_chunks

    @pl.when(c == 0)
    def _():
        ext[0:r, :] = prev_ref[...].reshape(r, dff)

    h = h_ref[...].reshape(tm, h_ref.shape[2])
    hb = h.astype(BF16)
    for ch in range(n_col_chunks):
        lo = ch * cwid
        ext[r:r + tm, lo:lo + cwid] = _dot(hb, wup[:, lo:lo + cwid])
        val = _dot(hb, wup[:, dff + lo:dff + lo + cwid])
        conv = cb[:, lo:lo + cwid] + ext[r:r + tm, lo:lo + cwid] * cw[width - 1:width, lo:lo + cwid]
        for j in range(1, width):
            conv = conv + ext[r - j * shift:r - j * shift + tm, lo:lo + cwid] * cw[width - 1 - j:width - j, lo:lo + cwid]
        gelu = 0.5 * conv * (1.0 + lax.erf(conv * math.sqrt(0.5)))
        act[:, lo:lo + cwid] = (gelu * val).astype(BF16)
    tail = ext[tm:tm + r, :]
    ext[0:r, :] = tail
    tail_o[...] = tail.reshape(tail_o.shape)
    o_ref[...] = _layer_norm(alpha * h + _dot(act[...], wdn[...]), lg[...], lb[...]).reshape(o_ref.shape)


def _ffn(h3, prev, p, alpha, time_major, tile):
    d = h3.shape[2]
    dff = p["ffn_conv_w"].shape[1]
    if time_major:
        nt, n_s, _ = h3.shape
        tile = min(tile, n_s)
        hblk, pblk = (nt, tile, d), (prev.shape[0], tile, dff)
        hmap = pmap = lambda b, c: (0, b, 0)
        grid, shift = (n_s // tile, 1), tile
    else:
        n, L, _ = h3.shape
        tile = min(tile, L)
        hblk, pblk = (1, tile, d), (1, prev.shape[1], dff)
        hmap, pmap = (lambda b, c: (b, c, 0)), (lambda b, c: (b, 0, 0))
        grid, shift = (n, L // tile), 1
    tm, r = hblk[0] * hblk[1], pblk[0] * pblk[1]
    consts = [p["w_ffn_up"], p["ffn_conv_w"], p["ffn_conv_b"], p["w_ffn_down"], p["ln2_g"], p["ln2_b"]]
    in_specs = [pl.BlockSpec(hblk, hmap), pl.BlockSpec(pblk, pmap)] + [_const_spec(cst.shape) for cst in consts]
    out_shape = [jax.ShapeDtypeStruct(h3.shape, F32), jax.ShapeDtypeStruct(prev.shape, F32)]
    out_specs = [pl.BlockSpec(hblk, hmap), pl.BlockSpec(pblk, pmap)]
    scratch = [pltpu.VMEM((tm + r, dff), F32), pltpu.VMEM((tm, dff), BF16)]
    return pl.pallas_call(
        functools.partial(_ffn_kernel, alpha, shift, 2),
        grid=grid, in_specs=in_specs, out_specs=out_specs, out_shape=out_shape, scratch_shapes=scratch,
        compiler_params=_params(("parallel", "arbitrary")), name="ffn",
    )(h3, prev, *consts)


def _ssd_sample_kernel(n_h, p_dim, d_state,
                       xbc_ref, prev_ref, dtf_ref, z_ref, st_ref, cw_ref, cb_ref, alog_ref, dskip_ref, nw_ref, e_ref,
                       y_o, st_o, xw_s, b_s, c_s, dec_s, yoff_s):
    nt, sb, cdim = xbc_ref.shape
    width = cw_ref.shape[0]
    d_inner = n_h * p_dim
    gw = N_GROUPS * d_state
    hpg = n_h // N_GROUPS
    gl = hpg * p_dim
    rows = nt * sb

    a_row = -jnp.exp(alog_ref[...])
    xin = [prev_ref[j] for j in range(width - 1)] + [xbc_ref[t] for t in range(nt)]
    xs, bm, cm, dts, acs = [], [], [], [], []
    run = None
    for t in range(nt):
        conv = cb_ref[...]
        for j in range(width):
            conv = conv + xin[t + j] * cw_ref[j:j + 1, :]
        act = _silu(conv)
        xs.append(act[:, :d_inner])
        bm.append(act[:, d_inner:d_inner + gw])
        cm.append(act[:, d_inner + gw:d_inner + 2 * gw])
        dtf = dtf_ref[t]
        lane = lax.broadcasted_iota(jnp.int32, dtf.shape, 1)
        dt = jnp.where(lane < n_h, dtf, 0.0)
        run = dt * a_row if run is None else run + dt * a_row
        dts.append(dt)
        acs.append(run)
    stacked = jnp.concatenate(dts + acs, axis=0)
    exp_all = _dot3_l(stacked, e_ref[...])
    dt_e = [exp_all[t * sb:(t + 1) * sb] for t in range(nt)]
    acs_e = [exp_all[(nt + t) * sb:(nt + t + 1) * sb] for t in range(nt)]
    xdt = [xs[t] * dt_e[t] for t in range(nt)]

    cbf = [cm[t].astype(BF16).astype(F32) for t in range(nt)]
    bbf = [bm[t].astype(BF16).astype(F32) for t in range(nt)]
    y_diag = []
    for t in range(nt):
        acc = jnp.zeros((sb, d_inner), F32)
        for s in range(t + 1):
            parts = []
            for g in range(N_GROUPS):
                dotg = jnp.sum(cbf[t][:, g * d_state:(g + 1) * d_state] * bbf[s][:, g * d_state:(g + 1) * d_state],
                               axis=-1, keepdims=True)
                w = jnp.exp(acs_e[t][:, g * gl:(g + 1) * gl] - acs_e[s][:, g * gl:(g + 1) * gl])
                parts.append(dotg * w * xdt[s][:, g * gl:(g + 1) * gl])
            acc = acc + jnp.concatenate(parts, axis=1)
        y_diag.append(acc)

    pad = xw_s.shape[0] - rows
    for t in range(nt):
        xw_s[t * sb:(t + 1) * sb, :] = xdt[t] * jnp.exp(acs_e[nt - 1] - acs_e[t])
        b_s[t * sb:(t + 1) * sb, :] = bm[t]
        c_s[t * sb:(t + 1) * sb, :] = cm[t]
    if pad:
        xw_s[rows:, :] = jnp.zeros((pad, d_inner), F32)
        b_s[rows:, :] = jnp.zeros((pad, gw), F32)
        c_s[rows:, :] = jnp.zeros((pad, gw), F32)
    dec_s[...] = jnp.exp(acs[nt - 1])
    yoff_s[...] = jnp.zeros(yoff_s.shape, F32)
    xw_t = xw_s[...].T.astype(BF16)
    rp = xw_s.shape[0]
    row_id = lax.broadcasted_iota(jnp.int32, (rp, 1), 0)

    def per_seq(j, carry):
        sel = (row_id % sb) == j
        h0 = st_ref[j]
        h0b = h0.astype(BF16)
        dec = dec_s[pl.ds(j, 1), :]
        bsel = jnp.where(sel, b_s[...], 0.0).astype(BF16)
        csel = jnp.where(sel, c_s[...], 0.0).astype(BF16)
        for g in range(N_GROUPS):
            yo = _dot_nt(csel[:, g * d_state:(g + 1) * d_state], h0b[g * gl:(g + 1) * gl, :])
            yoff_s[:, g * gl:(g + 1) * gl] = yoff_s[:, g * gl:(g + 1) * gl] + yo
            upd = _dot(xw_t[g * gl:(g + 1) * gl, :], bsel[:, g * d_state:(g + 1) * d_state])
            for r in range(hpg):
                h = g * hpg + r
                lo = h * p_dim
                dcol = jnp.broadcast_to(dec[:, h:h + 1], (p_dim, d_state))
                st_o[j, lo:lo + p_dim, :] = h0[lo:lo + p_dim, :] * dcol + upd[r * p_dim:(r + 1) * p_dim, :]
        return carry

    lax.fori_loop(0, sb, per_seq, 0)

    for t in range(nt):
        y = y_diag[t] + yoff_s[t * sb:(t + 1) * sb, :] * jnp.exp(acs_e[t]) + xs[t] * dskip_ref[...]
        y = y * _silu(z_ref[t].astype(F32))
        outs = []
        for g in range(N_GROUPS):
            yg = y[:, g * gl:(g + 1) * gl]
            outs.append(yg * lax.rsqrt(jnp.mean(yg * yg, axis=-1, keepdims=True) + RMS_EPS))
        y_o[t] = (jnp.concatenate(outs, axis=1) * nw_ref[...]).astype(y_o.dtype)


def _ssd_sample(xbc_t, prev_t, dtf_t, z_t, state, p, n_h, p_dim, d_state, sb):
    nt, s, cdim = xbc_t.shape
    sb = min(sb, s)
    d_inner = n_h * p_dim
    gw = N_GROUPS * d_state
    rows_pad = max(LANES, -(-nt * sb // LANES) * LANES)
    tb = lambda i: (0, i, 0)
    in_specs = [pl.BlockSpec((nt, sb, cdim), tb), pl.BlockSpec((prev_t.shape[0], sb, cdim), tb),
                pl.BlockSpec((nt, sb, LANES), tb), pl.BlockSpec((nt, sb, d_inner), tb),
                pl.BlockSpec((sb, d_inner, d_state), lambda i: (i, 0, 0)),
                _const_spec(p["conv_w"].shape), _const_spec((1, cdim)), _const_spec((1, LANES)),
                _const_spec((1, d_inner)), _const_spec((1, d_inner)), _const_spec((LANES, d_inner))]
    out_shape = [jax.ShapeDtypeStruct((nt, s, d_inner), BF16), jax.ShapeDtypeStruct(state.shape, F32)]
    out_specs = [pl.BlockSpec((nt, sb, d_inner), tb), pl.BlockSpec((sb, d_inner, d_state), lambda i: (i, 0, 0))]
    scratch = [pltpu.VMEM((rows_pad, d_inner), F32), pltpu.VMEM((rows_pad, gw), F32), pltpu.VMEM((rows_pad, gw), F32),
               pltpu.VMEM((sb, LANES), F32), pltpu.VMEM((rows_pad, d_inner), F32)]
    return pl.pallas_call(
        functools.partial(_ssd_sample_kernel, n_h, p_dim, d_state),
        grid=(s // sb,), in_specs=in_specs, out_specs=out_specs, out_shape=out_shape, scratch_shapes=scratch,
        compiler_params=_params(("parallel",)), name="ssd_sample",
    )(xbc_t, prev_t, dtf_t, z_t, state, p["conv_w"], p["conv_b"], p["a_log"], p["d_skip_e"], p["norm_w"], p["e_heads"])


def _fox_sample_kernel(n_pages, ch, page, n_heads, hd, nt,
                       pt_ref, qbd_ref, kn_ref, vn_ref, lfn_ref, su_ref, k_hbm, v_hbm, lf_hbm,
                       o_ref, kbuf, vbuf, lfbuf, ksem, vsem, lfsem):
    s = pl.program_id(0)
    ns = pl.num_programs(0)
    n_chunks = n_pages // ch
    rows = nt * n_heads
    da = n_heads * hd

    def kv_copies(seq, chunk, slot):
        cps = []
        for i in range(ch):
            pid = pt_ref[seq, chunk * ch + i]
            dst = pl.ds(i * page, page)
            cps.append(pltpu.make_async_copy(k_hbm.at[pid], kbuf.at[slot, :, dst], ksem.at[slot]))
            cps.append(pltpu.make_async_copy(v_hbm.at[pid], vbuf.at[slot, :, dst], vsem.at[slot]))
        return cps

    def lf_copies(seq, slot):
        return [pltpu.make_async_copy(lf_hbm.at[pt_ref[seq, i]], lfbuf.at[slot, i], lfsem.at[slot]) for i in range(n_pages)]

    @pl.when(s == 0)
    def _():
        for cp in lf_copies(0, 0):
            cp.start()
        for cp in kv_copies(0, 0, 0):
            cp.start()

    ls = s % 2
    for cp in lf_copies(s, ls):
        cp.wait()

    @pl.when(s + 1 < ns)
    def _():
        for cp in lf_copies(s + 1, 1 - ls):
            cp.start()

    lf = lfbuf[ls]
    lf2 = lf.reshape(n_pages * n_heads, page)
    within = _dot3_l(lf2, su_ref[...]).reshape(n_pages, n_heads, page)
    tot = jnp.sum(lf, axis=-1, keepdims=True)
    bias_pages = [None] * n_pages
    run = jnp.zeros((n_heads, 1), F32)
    for i in reversed(range(n_pages)):
        bias_pages[i] = within[i] + run
        run = run + tot[i]

    qbd = qbd_ref[0]
    m_run = jnp.full((rows, 1), NEG_BIG, F32)
    l_run = jnp.zeros((rows, 1), F32)
    acc = jnp.zeros((rows, da), F32)
    for c in range(n_chunks):
        g = s * n_chunks + c
        slot = g % 2
        if c + 1 < n_chunks:
            for cp in kv_copies(s, c + 1, 1 - slot):
                cp.start()
        else:
            @pl.when(s + 1 < ns)
            def _():
                for cp in kv_copies(s + 1, 0, 1 - slot):
                    cp.start()
        for cp in kv_copies(s, c, slot):
            cp.wait()
        kc = kbuf[slot].astype(BF16)
        vc = vbuf[slot].astype(BF16)
        bias = jnp.concatenate(bias_pages[c * ch:(c + 1) * ch], axis=1)
        u = jnp.concatenate([bias] * nt, axis=0) + _dot(qbd, kc)
        m_new = jnp.maximum(m_run, jnp.max(u, axis=-1, keepdims=True))
        alpha = jnp.exp(m_run - m_new)
        pm = jnp.exp(u - m_new)
        l_run = alpha * l_run + jnp.sum(pm, axis=-1, keepdims=True)
        acc = alpha * acc + _dot_nt(pm.astype(BF16), vc)
        m_run = m_new

    lfn = lfn_ref[0]
    lane = lax.broadcasted_iota(jnp.int32, lfn.shape, 1)
    cn = jnp.zeros(lfn.shape, F32)
    for t in range(nt):
        cn = cn + jnp.where(lane >= t, lfn[:, t:t + 1], 0.0)
    n_slots = kn_ref.shape[1]
    nbias = jnp.concatenate([-cn[:, :n_slots]] * nt, axis=0)
    row_t = lax.broadcasted_iota(jnp.int32, (rows, n_slots), 0) // n_heads
    lane_r = lax.broadcasted_iota(jnp.int32, (rows, n_slots), 1)
    u = jnp.where(lane_r <= row_t, nbias + _dot_nt(qbd, kn_ref[0]), NEG_BIG)
    m_new = jnp.maximum(m_run, jnp.max(u, axis=-1, keepdims=True))
    alpha = jnp.exp(m_run - m_new)
    pm = jnp.exp(u - m_new)
    l_run = alpha * l_run + jnp.sum(pm, axis=-1, keepdims=True)
    acc = alpha * acc + _dot(pm.astype(BF16), vn_ref[0])
    out = acc / l_run
    col_h = lax.broadcasted_iota(jnp.int32, (rows, da), 1) // hd
    row_h = lax.broadcasted_iota(jnp.int32, (rows, da), 0) % n_heads
    out = jnp.where(col_h == row_h, out, 0.0)
    o_ref[0] = jnp.concatenate(
        [jnp.sum(out[t * n_heads:(t + 1) * n_heads], axis=0, keepdims=True) for t in range(nt)]
        + [jnp.zeros((SUBLANES - nt, da), F32)], axis=0).astype(o_ref.dtype)


def _fox_sample(page_table, qbd, k_new, v_new, lf_new, cache_k, cache_v, cache_lft, n_heads, hd, nt, ch):
    s, n_pages = page_table.shape
    page = cache_k.shape[2]
    da = n_heads * hd
    ch = min(ch, n_pages)
    while n_pages % ch:
        ch -= 1
    rows = nt * n_heads
    su = jnp.triu(jnp.ones((page, page), F32), k=1).T.astype(BF16)
    per_seq = lambda i, pt: (i, 0, 0)
    grid_spec = pltpu.PrefetchScalarGridSpec(
        num_scalar_prefetch=1, grid=(s,),
        in_specs=[pl.BlockSpec((1, rows, da), per_seq), pl.BlockSpec((1, k_new.shape[1], da), per_seq),
                  pl.BlockSpec((1, k_new.shape[1], da), per_seq), pl.BlockSpec((1, n_heads, LANES), per_seq),
                  pl.BlockSpec((page, page), lambda i, pt: (0, 0)),
                  pl.BlockSpec(memory_space=pl.ANY), pl.BlockSpec(memory_space=pl.ANY), pl.BlockSpec(memory_space=pl.ANY)],
        out_specs=pl.BlockSpec((1, SUBLANES, da), per_seq),
        scratch_shapes=[pltpu.VMEM((2, da, ch * page), F32), pltpu.VMEM((2, da, ch * page), F32),
                        pltpu.VMEM((2, n_pages, n_heads, page), F32),
                        pltpu.SemaphoreType.DMA((2,)), pltpu.SemaphoreType.DMA((2,)), pltpu.SemaphoreType.DMA((2,))])
    return pl.pallas_call(
        functools.partial(_fox_sample_kernel, n_pages, ch, page, n_heads, hd, nt),
        grid_spec=grid_spec, out_shape=jax.ShapeDtypeStruct((s, SUBLANES, da), BF16),
        compiler_params=_params(("arbitrary",)), name="fox_sample",
    )(page_table, qbd, k_new, v_new, lf_new, su, cache_k, cache_v, cache_lft)


def _prep_params(dims, w_in, conv_ssm_w, conv_ssm_b, dt_bias, a_log, d_skip, ssm_norm_w, f_bias, gate_bias,
                 w_ssm_o, w_att_o, w_o, ln1_g, ln1_b, w_ffn_up, ffn_conv_w, ffn_conv_b, w_ffn_down, ln2_g, ln2_b):
    d_inner, cdim, n_h, d_att, n_att, d_model, p_dim, hd = dims
    z_end = d_inner
    xbc_end = z_end + cdim
    dt_end = xbc_end + n_h
    q_end = dt_end + d_att
    k_end = q_end + d_att
    v_end = k_end + d_att
    f_end = v_end + n_att
    row = lambda a: a.reshape(1, -1).astype(F32)
    pad_lanes = lambda a: jnp.pad(a, ((0, 0), (0, LANES - a.shape[1])))
    scale = hd ** -0.5
    w = {
        "n_dt": n_h, "n_att": n_att,
        "z": w_in[:, :z_end].astype(BF16),
        "xbc": w_in[:, z_end:xbc_end].astype(BF16),
        "q": (w_in[:, dt_end:q_end] * scale).astype(BF16),
        "q_log2": (w_in[:, dt_end:q_end] * (scale * LOG2E)).astype(BF16),
        "k": w_in[:, q_end:k_end].astype(BF16),
        "v": w_in[:, k_end:v_end].astype(BF16),
        "g": w_in[:, f_end:].astype(BF16),
        "sm": pad_lanes(jnp.concatenate([w_in[:, xbc_end:dt_end], w_in[:, v_end:f_end]], axis=1)).astype(BF16),
        "bsm": pad_lanes(jnp.concatenate([row(dt_bias), row(f_bias)], axis=1)),
    }
    head_of_lane = jnp.arange(d_inner) // p_dim
    p = {
        "conv_w": conv_ssm_w.astype(F32), "conv_b": row(conv_ssm_b),
        "a_log": pad_lanes(row(a_log)),
        "d_skip_e": row(jnp.repeat(d_skip, p_dim)), "norm_w": row(ssm_norm_w),
        "e_heads": (jnp.arange(LANES)[:, None] == head_of_lane[None, :]).astype(BF16),
        "w_ssm_o": w_ssm_o.astype(BF16), "w_att_o": w_att_o.astype(BF16), "w_o": w_o.astype(BF16),
        "gate_bias": row(gate_bias), "ln1_g": row(ln1_g), "ln1_b": row(ln1_b),
        "w_ffn_up": w_ffn_up.astype(BF16), "ffn_conv_w": ffn_conv_w.astype(F32), "ffn_conv_b": row(ffn_conv_b),
        "w_ffn_down": w_ffn_down.astype(BF16), "ln2_g": row(ln2_g), "ln2_b": row(ln2_b),
    }
    return w, p


def _forward(x_prompt, x_sample, cache_k, cache_v, cache_logf, state_conv_ssm, state_ssm, state_conv_ffn, page_table,
             w_in, conv_ssm_w, conv_ssm_b, dt_bias, a_log, d_skip, ssm_norm_w, f_bias, gate_bias, w_ssm_o, w_att_o,
             w_o, ln1_g, ln1_b, w_ffn_up, ffn_conv_w, ffn_conv_b, w_ffn_down, ln2_g, ln2_b,
             tm_proj=256, tm_merge=512, tm_ffn=256, att_blk=2048, att_tk=512, sb=8, sbf=32, ch=16):
    depth = w_in.shape[0]
    assert depth == 1, "single-layer step"
    n_p, L, d_model = x_prompt.shape
    n_s, nt, _ = x_sample.shape
    n_pool, page, n_att, hd = cache_k.shape[1:]
    n_h, p_dim, d_state = state_ssm.shape[2:]
    d_inner = n_h * p_dim
    cdim = conv_ssm_w.shape[2]
    d_att = n_att * hd
    dff = ffn_conv_w.shape[2]
    ssm_w = conv_ssm_w.shape[1]
    ffn_w = ffn_conv_w.shape[1]
    alpha = (2.0 * depth) ** 0.25
    dims = (d_inner, cdim, n_h, d_att, n_att, d_model, p_dim, hd)
    lyr = 0
    w, p = _prep_params(dims, w_in[lyr], conv_ssm_w[lyr], conv_ssm_b[lyr], dt_bias[lyr], a_log[lyr], d_skip[lyr],
                        ssm_norm_w[lyr], f_bias[lyr], gate_bias[lyr], w_ssm_o[lyr], w_att_o[lyr], w_o[lyr], ln1_g[lyr],
                        ln1_b[lyr], w_ffn_up[lyr], ffn_conv_w[lyr], ffn_conv_b[lyr], w_ffn_down[lyr], ln2_g[lyr], ln2_b[lyr])

    xp2 = x_prompt.reshape(n_p * L, d_model)
    wp = dict(w, q=w["q_log2"])
    z, xbc, _, qt, kb, kt, _, vt, vtb, gates, dtf, lft = _in_proj(x_prompt, wp, tm_proj)
    xbc3 = xbc.reshape(n_p, L, cdim)
    y_ssm, kx, h_fin = _ssd_prompt(xbc3, dtf.reshape(n_p, L, LANES), z.reshape(n_p, L, d_inner), kb.reshape(n_p, L, d_att),
                                   p, n_h, p_dim, d_state, n_att)
    att_t = _fox_prompt(qt, kx, vtb, hd, att_blk, att_tk)
    h1 = _merge(xp2, y_ssm.reshape(n_p * L, d_inner), att_t, gates, p, alpha, tm_merge)
    prev0 = jnp.zeros((n_p, SUBLANES, dff), F32)
    y_p, tail_p = _ffn(h1.reshape(n_p, L, d_model), prev0, p, alpha, False, tm_ffn)
    k_p = jnp.transpose(kt.reshape(1, n_p, n_att, hd, L), (0, 1, 4, 2, 3))
    v_p = jnp.transpose(vt.reshape(1, n_p, n_att, hd, L), (0, 1, 4, 2, 3))
    lf_p = jnp.transpose(lft.reshape(1, n_p, n_att, L), (0, 1, 3, 2))
    cs_p = xbc3[:, L - (ssm_w - 1):, :][None]
    ss_p = h_fin.reshape(1, n_p, n_h, p_dim, d_state)
    cf_p = tail_p[:, SUBLANES - (ffn_w - 1):, :][None]

    xs3 = jnp.transpose(x_sample, (1, 0, 2))
    xs2 = xs3.reshape(nt * n_s, d_model)
    z, xbc, q, _, kb, kt, vb, vt, _, gates, dtf, lft = _in_proj(xs3, w, tm_proj)
    xbc_t = xbc.reshape(nt, n_s, cdim)
    prev_t = jnp.transpose(state_conv_ssm[lyr], (1, 0, 2))
    y_ssm, ss_new = _ssd_sample(xbc_t, prev_t, dtf.reshape(nt, n_s, LANES), z.reshape(nt, n_s, d_inner),
                                state_ssm[lyr].reshape(n_s, d_inner, d_state), p, n_h, p_dim, d_state, sb)
    q_t = q.reshape(nt, n_s, n_att, hd)
    eye = jnp.eye(n_att, dtype=BF16)
    qbd = jnp.einsum("tshe,hg->sthge", q_t, eye).reshape(n_s, nt * n_att, d_att)
    seq_major = lambda a: jnp.pad(jnp.transpose(a.reshape(nt, n_s, d_att), (1, 0, 2)), ((0, 0), (0, NEW_KEY_SLOTS - nt), (0, 0)))
    lf_new = jnp.transpose(dtf[:, n_h:n_h + n_att].reshape(nt, n_s, n_att), (1, 2, 0))
    lf_new = jnp.pad(lf_new, ((0, 0), (0, 0), (0, LANES - nt)))
    att = _fox_sample(page_table, qbd, seq_major(kb), seq_major(vb), lf_new,
                      jnp.transpose(cache_k[lyr], (0, 2, 3, 1)).reshape(n_pool, d_att, page),
                      jnp.transpose(cache_v[lyr], (0, 2, 3, 1)).reshape(n_pool, d_att, page),
                      jnp.transpose(cache_logf[lyr], (0, 2, 1)), n_att, hd, nt, ch)
    att_t = jnp.transpose(att[:, :nt, :], (1, 0, 2)).reshape(nt * n_s, d_att)
    h1 = _merge(xs2, y_ssm.reshape(nt * n_s, d_inner), att_t, gates, p, alpha, tm_merge)
    prev_f = jnp.transpose(state_conv_ffn[lyr], (1, 0, 2))
    y_s, tail_s = _ffn(h1.reshape(nt, n_s, d_model), prev_f, p, alpha, True, sbf)
    back = lambda a, *tail: jnp.transpose(a.reshape(nt, n_s, *tail), (1, 0) + tuple(range(2, 2 + len(tail))))
    y_s = back(y_s, d_model)
    k_s = jnp.transpose(kt.reshape(1, nt, n_att, hd, n_s), (0, 4, 1, 2, 3))
    v_s = jnp.transpose(vt.reshape(1, nt, n_att, hd, n_s), (0, 4, 1, 2, 3))
    lf_s = jnp.transpose(lft.reshape(1, nt, n_att, n_s), (0, 3, 1, 2))
    cs_s = jnp.transpose(jnp.concatenate([prev_t, xbc_t], axis=0)[-(ssm_w - 1):], (1, 0, 2))[None]
    ss_s = ss_new.reshape(1, n_s, n_h, p_dim, d_state)
    cf_s = jnp.transpose(tail_s, (1, 0, 2))[None]
    y_p = y_p.reshape(n_p, L, d_model)
    return (y_p, y_s, k_p, v_p, lf_p, cs_p, ss_p, cf_p, k_s, v_s, lf_s, cs_s, ss_s, cf_s)


def kernel(x_prompt, x_sample, cache_k, cache_v, cache_logf, state_conv_ssm, state_ssm, state_conv_ffn, page_table, w_in, conv_ssm_w, conv_ssm_b, dt_bias, a_log, d_skip, ssm_norm_w, f_bias, gate_bias, w_ssm_o, w_att_o, w_o, ln1_g, ln1_b, w_ffn_up, ffn_conv_w, ffn_conv_b, w_ffn_down, ln2_g, ln2_b):
    return _forward(x_prompt, x_sample, cache_k, cache_v, cache_logf, state_conv_ssm, state_ssm, state_conv_ffn, page_table,
                    w_in, conv_ssm_w, conv_ssm_b, dt_bias, a_log, d_skip, ssm_norm_w, f_bias, gate_bias, w_ssm_o, w_att_o,
                    w_o, ln1_g, ln1_b, w_ffn_up, ffn_conv_w, ffn_conv_b, w_ffn_down, ln2_g, ln2_b)
```
